```python
import math
import jax, jax.numpy as jnp
from jax import lax
import numpy as np

D_MODEL = 1024
BATCH = 4
SEQ = 8192
DEPTH = 2

ATTN_HEADS = 4
HEAD_DIM = 64
ATTN_W = ATTN_HEADS * 2 * HEAD_DIM
ROT_DIM = HEAD_DIM // 4
ROPE_THETA = 500000.0
Q_BLOCK = 128
CONV_W = 512
CONV_K = 31
D_FF = ((int(math.ceil(8 * D_MODEL / 3)) + 255) // 256) * 256
N_BRANCH = 2
Q_OFF = 0
K_OFF = Q_OFF + ATTN_W
V_OFF = K_OFF + ATTN_W
C_OFF = V_OFF + ATTN_W
G_OFF = C_OFF + 2 * CONV_W
N_IN = G_OFF + N_BRANCH * D_MODEL
N_MOD = 6

kernel_name = "gated_parallel_diffattn_conformer_hybrid"


def rms_norm(x, g, eps=1e-6):
    xf = x.astype(jnp.float32)
    y = xf * lax.rsqrt(jnp.mean(xf * xf, axis=-1, keepdims=True) + eps)
    return (y * g.astype(jnp.float32)).astype(x.dtype)


def layer_norm(x, g, b, eps=1e-5):
    xf = x.astype(jnp.float32)
    mu = jnp.mean(xf, axis=-1, keepdims=True)
    var = jnp.mean(jnp.square(xf - mu), axis=-1, keepdims=True)
    y = (xf - mu) * lax.rsqrt(var + eps)
    return (y * g.astype(jnp.float32) + b.astype(jnp.float32)).astype(x.dtype)


def partial_rope(t, cos, sin):
    half = ROT_DIM // 2
    x1 = t[..., :half]
    x2 = t[..., half:ROT_DIM]
    rot = jnp.concatenate([x1 * cos - x2 * sin, x2 * cos + x1 * sin], axis=-1)
    return jnp.concatenate([rot, t[..., ROT_DIM:]], axis=-1)


def diff_attention(q, k, v, lam):
    b, s, h, _, d = q.shape
    scale = 1.0 / math.sqrt(d)
    qt = q.transpose(0, 2, 3, 1, 4)
    kt = k.transpose(0, 2, 3, 1, 4)
    vt = v.transpose(0, 2, 1, 3)
    k_pos = jnp.arange(s)
    neg = jnp.finfo(jnp.float32).min

    def block(i):
        start = i * Q_BLOCK
        qb = lax.dynamic_slice_in_dim(qt, start, Q_BLOCK, axis=3)
        sc = jnp.einsum('bhcqd,bhckd->bhcqk', qb, kt).astype(jnp.float32) * scale
        q_pos = start + jnp.arange(Q_BLOCK)
        mask = q_pos[:, None] >= k_pos[None, :]
        sc = jnp.where(mask, sc, neg)
        p = jax.nn.softmax(sc, axis=-1)
        a = p[:, :, 0] - lam * p[:, :, 1]
        return jnp.einsum('bhqk,bhkv->bhqv', a.astype(vt.dtype), vt)

    out = lax.map(block, jnp.arange(s // Q_BLOCK))
    out = out.transpose(1, 0, 3, 2, 4)
    return out.reshape(b, s, h, 2 * d)


def causal_depthwise_conv(u, w, bias):
    y = lax.conv_general_dilated(
        u, w.astype(u.dtype), window_strides=(1,), padding=[(CONV_K - 1, 0)],
        dimension_numbers=('NWC', 'WIO', 'NWC'), feature_group_count=u.shape[-1])
    return y + bias


def setup_inputs(seed: int = 0) -> dict:
    key = jax.random.key(seed)
    ks = jax.random.split(key, 24)
    f32 = jnp.float32

    def nrm(k, shape, s):
        return jax.random.normal(k, shape, f32) * s

    L = DEPTH
    return {
        "x": nrm(ks[0], (BATCH, SEQ, D_MODEL), 1.0),
        "c": nrm(ks[1], (BATCH, D_MODEL), 1.0),
        "positions": jnp.broadcast_to(jnp.arange(SEQ, dtype=jnp.int32)[None, :], (BATCH, SEQ)),
        "ada_w": nrm(ks[2], (L, D_MODEL, N_MOD * D_MODEL), 0.1 * D_MODEL ** -0.5),
        "ada_b": nrm(ks[3], (L, N_MOD * D_MODEL), 0.01),
        "norm1_g": 1.0 + nrm(ks[4], (L, D_MODEL), 0.01),
        "w_in": nrm(ks[5], (L, D_MODEL, N_IN), D_MODEL ** -0.5),
        "lambda_q1": nrm(ks[6], (L, HEAD_DIM), 0.1),
        "lambda_k1": nrm(ks[7], (L, HEAD_DIM), 0.1),
        "lambda_q2": nrm(ks[8], (L, HEAD_DIM), 0.1),
        "lambda_k2": nrm(ks[9], (L, HEAD_DIM), 0.1),
        "subln_g": 1.0 + nrm(ks[10], (L, 2 * HEAD_DIM), 0.01),
        "w_attn_o": nrm(ks[11], (L, ATTN_W, D_MODEL), ATTN_W ** -0.5),
        "dw_conv_w": nrm(ks[12], (L, CONV_K, 1, CONV_W), CONV_K ** -0.5),
        "dw_conv_b": nrm(ks[13], (L, CONV_W), 0.01),
        "conv_ln_g": 1.0 + nrm(ks[14], (L, CONV_W), 0.01),
        "conv_ln_b": nrm(ks[15], (L, CONV_W), 0.01),
        "w_conv_o": nrm(ks[16], (L, CONV_W, D_MODEL), CONV_W ** -0.5),
        "w_out": nrm(ks[17], (L, D_MODEL, D_MODEL), D_MODEL ** -0.5),
        "norm2_g": 1.0 + nrm(ks[18], (L, D_MODEL), 0.01),
        "w_ffn_in": nrm(ks[19], (L, D_MODEL, 2 * D_FF), D_MODEL ** -0.5),
        "w_ffn_out": nrm(ks[20], (L, D_FF, D_MODEL), D_FF ** -0.5),
        "final_g": 1.0 + nrm(ks[21], (D_MODEL,), 0.01),
    }


def reference(x, c, positions, ada_w, ada_b, norm1_g, w_in, lambda_q1, lambda_k1,
              lambda_q2, lambda_k2, subln_g, w_attn_o, dw_conv_w, dw_conv_b,
              conv_ln_g, conv_ln_b, w_conv_o, w_out, norm2_g, w_ffn_in, w_ffn_out,
              final_g):
    b, s, _ = x.shape
    inv_freq = 1.0 / (ROPE_THETA ** (jnp.arange(0, ROT_DIM, 2, dtype=jnp.float32) / ROT_DIM))
    ang = positions.astype(jnp.float32)[..., None] * inv_freq
    cos = jnp.cos(ang)[:, :, None, None, :].astype(x.dtype)
    sin = jnp.sin(ang)[:, :, None, None, :].astype(x.dtype)
    c_act = jax.nn.silu(c)

    for l in range(DEPTH):
        mod = c_act @ ada_w[l] + ada_b[l]
        shift1, scale1, gate1, shift2, scale2, gate2 = [
            m[:, None, :] for m in jnp.split(mod, N_MOD, axis=-1)]

        h = rms_norm(x, norm1_g[l]) * (1.0 + scale1) + shift1
        z = h @ w_in[l]

        q = z[..., Q_OFF:K_OFF].reshape(b, s, ATTN_HEADS, 2, HEAD_DIM)
        k = z[..., K_OFF:V_OFF].reshape(b, s, ATTN_HEADS, 2, HEAD_DIM)
        v = z[..., V_OFF:C_OFF].reshape(b, s, ATTN_HEADS, 2 * HEAD_DIM)
        q = partial_rope(q, cos, sin)
        k = partial_rope(k, cos, sin)
        lambda_init = 0.8 - 0.6 * math.exp(-0.3 * l)
        lam = (jnp.exp(jnp.sum(lambda_q1[l].astype(jnp.float32) * lambda_k1[l].astype(jnp.float32)))
               - jnp.exp(jnp.sum(lambda_q2[l].astype(jnp.float32) * lambda_k2[l].astype(jnp.float32)))
               + lambda_init)
        o = diff_attention(q, k, v, lam)
        o = rms_norm(o, subln_g[l], eps=1e-5) * (1.0 - lambda_init)
        y_attn = o.reshape(b, s, ATTN_W) @ w_attn_o[l]

        ga, gb = jnp.split(z[..., C_OFF:G_OFF], 2, axis=-1)
        u = ga * jax.nn.sigmoid(gb)
        u = causal_depthwise_conv(u, dw_conv_w[l], dw_conv_b[l])
        u = jax.nn.silu(layer_norm(u, conv_ln_g[l], conv_ln_b[l]))
        y_conv = u @ w_conv_o[l]

        g_attn, g_conv = jnp.split(jax.nn.sigmoid(z[..., G_OFF:]), N_BRANCH, axis=-1)
        mixed = (g_attn * y_attn + g_conv * y_conv) @ w_out[l]
        x = x + (1.0 + gate1) * mixed

        h2 = rms_norm(x, norm2_g[l]) * (1.0 + scale2) + shift2
        f_gate, f_up = jnp.split(h2 @ w_ffn_in[l], 2, axis=-1)
        x = x + (1.0 + gate2) * ((jax.nn.silu(f_gate) * f_up) @ w_ffn_out[l])

    return rms_norm(x, final_g)
```

```python
import functools
import math

import jax
import jax.numpy as jnp
from jax import lax
from jax.experimental import pallas as pl
from jax.experimental.pallas import tpu as pltpu

F32 = jnp.float32
BF16 = jnp.bfloat16

ATTN_HEADS = 4
HEAD_DIM = 64
HEAD_W = 2 * HEAD_DIM
ATTN_W = ATTN_HEADS * HEAD_W
ROT_DIM = HEAD_DIM // 4
ROPE_THETA = 500000.0
CONV_W = 512
CONV_K = 31
N_MOD = 6
N_QKVU = 3 * ATTN_W + 2 * CONV_W
CONV_HALO = 32
MASK_VALUE = -1e30

V7X_VMEM_LIMIT_BYTES = 56 * 1024 * 1024

ROW_TILE = 512
ATTN_TILE = 512
ADA_COLS = 1024
FFN_CHUNK = 1024


def _sigmoid(x):
    return jax.nn.sigmoid(x)


def _modulated_rms_norm(x, g, scale, shift, eps=1e-6):
    y = x * lax.rsqrt(jnp.mean(x * x, axis=-1, keepdims=True) + eps)
    return (y * g) * (1.0 + scale) + shift


def _resident(shape):
    zeros = (0,) * len(shape)
    return pl.BlockSpec(shape, lambda *_: zeros, pipeline_mode=pl.Buffered(1))


def _ada_kernel(c_ref, w_ref, b_ref, o_ref):
    c = c_ref[...]
    c_act = (c * _sigmoid(c)).astype(BF16)
    w = w_ref[0].astype(BF16)
    o_ref[0] = jnp.dot(c_act, w, preferred_element_type=F32) + b_ref[0]


def _ada_call(c_pad, ada_w, ada_b):
    depth, d, n = ada_w.shape
    rows = c_pad.shape[0]
    return pl.pallas_call(
        _ada_kernel,
        grid=(depth, n // ADA_COLS),
        in_specs=[
            pl.BlockSpec((rows, d), lambda l, j: (0, 0)),
            pl.BlockSpec((1, d, ADA_COLS), lambda l, j: (l, 0, j)),
            pl.BlockSpec((1, 1, ADA_COLS), lambda l, j: (l, 0, j)),
        ],
        out_specs=pl.BlockSpec((1, rows, ADA_COLS), lambda l, j: (l, 0, j)),
        out_shape=jax.ShapeDtypeStruct((depth, rows, n), F32),
        compiler_params=pltpu.CompilerParams(
            dimension_semantics=("arbitrary", "arbitrary"),
            vmem_limit_bytes=V7X_VMEM_LIMIT_BYTES),
        name="ada_proj",
    )(c_pad, ada_w, ada_b.reshape(depth, 1, n))


def _inproj_kernel(x_ref, mod_ref, g_ref, w_ref, rc_ref, rs1_ref, rs2_ref,
                   q_ref, k_ref, v_ref, u_ref):
    x = x_ref[0]
    h = _modulated_rms_norm(x, g_ref[...], mod_ref[0, 1:2, :], mod_ref[0, 0:1, :]).astype(BF16)

    rc = rc_ref[0]
    rs1 = rs1_ref[0]
    rs2 = rs2_ref[0]

    def rope(t):
        return t * rc + pltpu.roll(t, HEAD_W - ROT_DIM // 2, 1) * rs1 + pltpu.roll(t, ROT_DIM // 2, 1) * rs2

    zq = jnp.dot(h, w_ref[:, 0:ATTN_W], preferred_element_type=F32)
    for hd in range(ATTN_HEADS):
        t = rope(zq[:, hd * HEAD_W:(hd + 1) * HEAD_W])
        q_ref[0, hd] = (t * (1.0 / math.sqrt(HEAD_DIM))).astype(BF16)

    zk = jnp.dot(h, w_ref[:, ATTN_W:2 * ATTN_W], preferred_element_type=F32)
    for hd in range(ATTN_HEADS):
        k_ref[0, hd] = rope(zk[:, hd * HEAD_W:(hd + 1) * HEAD_W]).astype(BF16)

    zv = jnp.dot(h, w_ref[:, 2 * ATTN_W:3 * ATTN_W], preferred_element_type=F32)
    for hd in range(ATTN_HEADS):
        v_ref[0, hd] = zv[:, hd * HEAD_W:(hd + 1) * HEAD_W].astype(BF16)

    c0 = 3 * ATTN_W
    ga = jnp.dot(h, w_ref[:, c0:c0 + CONV_W], preferred_element_type=F32)
    gb = jnp.dot(h, w_ref[:, c0 + CONV_W:c0 + 2 * CONV_W], preferred_element_type=F32)
    u_ref[0] = ga * _sigmoid(gb)


def _inproj_call(x, mod, g, w, rc, rs1, rs2):
    b, s, d = x.shape
    tm = min(ROW_TILE, s)
    row = lambda bi, i: (bi, i, 0)
    head_spec = pl.BlockSpec((1, ATTN_HEADS, tm, HEAD_W), lambda bi, i: (bi, 0, i, 0))
    head_shape = jax.ShapeDtypeStruct((b, ATTN_HEADS, s, HEAD_W), BF16)
    return pl.pallas_call(
        _inproj_kernel,
        grid=(b, s // tm),
        in_specs=[
            pl.BlockSpec((1, tm, d), row),
            pl.BlockSpec((1, N_MOD, d), lambda bi, i: (bi, 0, 0)),
            _resident((1, d)),
            _resident((d, N_QKVU)),
            pl.BlockSpec((1, tm, HEAD_W), row),
            pl.BlockSpec((1, tm, HEAD_W), row),
            pl.BlockSpec((1, tm, HEAD_W), row),
        ],
        out_specs=[head_spec, head_spec, head_spec, pl.BlockSpec((1, tm, CONV_W), row)],
        out_shape=[head_shape, head_shape, head_shape, jax.ShapeDtypeStruct((b, s, CONV_W), F32)],
        compiler_params=pltpu.CompilerParams(
            dimension_semantics=("arbitrary", "arbitrary"),
            vmem_limit_bytes=V7X_VMEM_LIMIT_BYTES),
        name="in_proj",
    )(x, mod, g, w, rc, rs1, rs2)


def _attn_kernel(q_ref, k_ref, v_ref, lq1_ref, lk1_ref, lq2_ref, lk2_ref, g_ref, o_ref,
                 qbd_scr, m_scr, l_scr, acc_scr, *, tile, lambda_init):
    i = pl.program_id(2)

    q = q_ref[0, 0]
    first = lax.broadcasted_iota(jnp.int32, q.shape, 1) < HEAD_DIM
    zero = jnp.zeros_like(q)
    qbd_scr[0:tile, :] = jnp.where(first, q, zero)
    qbd_scr[tile:2 * tile, :] = jnp.where(first, zero, q)

    m_scr[...] = jnp.full(m_scr.shape, MASK_VALUE, F32)
    l_scr[...] = jnp.zeros(l_scr.shape, F32)
    acc_scr[...] = jnp.zeros(acc_scr.shape, F32)

    def step(j, masked):
        start = pl.multiple_of(j * tile, tile)
        k = k_ref[0, 0, pl.ds(start, tile), :]
        v = v_ref[0, 0, pl.ds(start, tile), :]
        s = lax.dot_general(qbd_scr[...], k, (((1,), (1,)), ((), ())), preferred_element_type=F32)
        if masked:
            r = lax.broadcasted_iota(jnp.int32, (tile, tile), 0)
            c = lax.broadcasted_iota(jnp.int32, (tile, tile), 1)
            keep = jnp.concatenate([r >= c, r >= c], axis=0)
            s = jnp.where(keep, s, MASK_VALUE)
        m_prev = m_scr[...]
        m_new = jnp.maximum(m_prev, jnp.max(s, axis=-1, keepdims=True))
        alpha = jnp.exp(m_prev - m_new)
        p = jnp.exp(s - m_new)
        l_scr[...] = alpha * l_scr[...] + jnp.sum(p, axis=-1, keepdims=True)
        acc_scr[...] = alpha * acc_scr[...] + jnp.dot(p.astype(BF16), v, preferred_element_type=F32)
        m_scr[...] = m_new

    def body(j, carry):
        step(j, False)
        return carry

    lax.fori_loop(0, i, body, 0)
    step(i, True)

    lam = (jnp.exp(jnp.sum(lq1_ref[...] * lk1_ref[...], axis=-1, keepdims=True))
           - jnp.exp(jnp.sum(lq2_ref[...] * lk2_ref[...], axis=-1, keepdims=True))
           + lambda_init)
    o1 = acc_scr[0:tile, :] / l_scr[0:tile, :]
    o2 = acc_scr[tile:2 * tile, :] / l_scr[tile:2 * tile, :]
    o = o1 - lam * o2
    o = o * lax.rsqrt(jnp.mean(o * o, axis=-1, keepdims=True) + 1e-5)
    o_ref[0] = ((o * g_ref[...]) * (1.0 - lambda_init)).astype(BF16)


def _attn_call(q, k, v, lq1, lk1, lq2, lk2, subln_g, lambda_init):
    b, nh, s, _ = q.shape
    tile = min(ATTN_TILE, s)
    kv_spec = pl.BlockSpec((1, 1, s, HEAD_W), lambda bi, hi, i: (bi, hi, 0, 0))
    vec = lambda n: pl.BlockSpec((1, n), lambda bi, hi, i: (0, 0))
    return pl.pallas_call(
        functools.partial(_attn_kernel, tile=tile, lambda_init=lambda_init),
        grid=(b, nh, s // tile),
        in_specs=[
            pl.BlockSpec((1, 1, tile, HEAD_W), lambda bi, hi, i: (bi, hi, i, 0)),
            kv_spec, kv_spec,
            vec(HEAD_DIM), vec(HEAD_DIM), vec(HEAD_DIM), vec(HEAD_DIM), vec(HEAD_W),
        ],
        out_specs=pl.BlockSpec((1, tile, HEAD_W), lambda bi, hi, i: (bi, i, hi)),
        out_shape=jax.ShapeDtypeStruct((b, s, nh * HEAD_W), BF16),
        scratch_shapes=[
            pltpu.VMEM((2 * tile, HEAD_W), BF16),
            pltpu.VMEM((2 * tile, 1), F32),
            pltpu.VMEM((2 * tile, 1), F32),
            pltpu.VMEM((2 * tile, HEAD_W), F32),
        ],
        compiler_params=pltpu.CompilerParams(
            dimension_semantics=("arbitrary", "arbitrary", "arbitrary"),
            vmem_limit_bytes=V7X_VMEM_LIMIT_BYTES),
        name="diff_attn",
    )(q, k, v, lq1, lk1, lq2, lk2, subln_g)


def _mixer_kernel(x_ref, mod_ref, g_ref, wg_ref, o_ref, u_ref, uh_ref, cw_ref, cb_ref, lg_ref, lb_ref,
                  wao_ref, wco_ref, wout_ref, x1_ref, ubuf, *, tm):
    i = pl.program_id(1)
    d = x_ref.shape[-1]
    x = x_ref[0]
    h = _modulated_rms_norm(x, g_ref[...], mod_ref[0, 1:2, :], mod_ref[0, 0:1, :]).astype(BF16)
    gates = _sigmoid(jnp.dot(h, wg_ref[...], preferred_element_type=F32))

    halo = uh_ref[0]
    ubuf[0:CONV_HALO, :] = jnp.where(i > 0, halo, jnp.zeros_like(halo))
    ubuf[CONV_HALO:CONV_HALO + tm, :] = u_ref[0]
    base = CONV_HALO - (CONV_K - 1)
    conv = cb_ref[...] + cw_ref[0:1, :] * ubuf[base:base + tm, :]
    for j in range(1, CONV_K):
        conv = conv + cw_ref[j:j + 1, :] * ubuf[base + j:base + j + tm, :]

    mu = jnp.mean(conv, axis=-1, keepdims=True)
    cen = conv - mu
    var = jnp.mean(cen * cen, axis=-1, keepdims=True)
    y = (cen * lax.rsqrt(var + 1e-5)) * lg_ref[...] + lb_ref[...]
    act = (y * _sigmoid(y)).astype(BF16)

    y_conv = jnp.dot(act, wco_ref[...], preferred_element_type=F32)
    y_attn = jnp.dot(o_ref[0], wao_ref[...], preferred_element_type=F32)
    mix = (gates[:, 0:d] * y_attn + gates[:, d:2 * d] * y_conv).astype(BF16)
    mixed = jnp.dot(mix, wout_ref[...], preferred_element_type=F32)
    x1_ref[0] = x + (1.0 + mod_ref[0, 2:3, :]) * mixed


def _mixer_call(x, mod, g, wg, o, u, cw, cb, lg, lb, wao, wco, wout):
    b, s, d = x.shape
    tm = min(ROW_TILE, s)
    halo_blocks = tm // CONV_HALO
    row = lambda bi, i: (bi, i, 0)
    return pl.pallas_call(
        functools.partial(_mixer_kernel, tm=tm),
        grid=(b, s // tm),
        in_specs=[
            pl.BlockSpec((1, tm, d), row),
            pl.BlockSpec((1, N_MOD, d), lambda bi, i: (bi, 0, 0)),
            _resident((1, d)),
            _resident(wg.shape),
            pl.BlockSpec((1, tm, ATTN_W), row),
            pl.BlockSpec((1, tm, CONV_W), row),
            pl.BlockSpec((1, CONV_HALO, CONV_W), lambda bi, i: (bi, jnp.maximum(i * halo_blocks - 1, 0), 0)),
            _resident(cw.shape), _resident(cb.shape), _resident(lg.shape), _resident(lb.shape),
            _resident(wao.shape), _resident(wco.shape), _resident(wout.shape),
        ],
        out_specs=pl.BlockSpec((1, tm, d), row),
        out_shape=jax.ShapeDtypeStruct((b, s, d), F32),
        scratch_shapes=[pltpu.VMEM((CONV_HALO + tm, CONV_W), F32)],
        compiler_params=pltpu.CompilerParams(
            dimension_semantics=("arbitrary", "arbitrary"),
            vmem_limit_bytes=V7X_VMEM_LIMIT_BYTES),
        name="mixer",
    )(x, mod, g, wg, o, u, u, cw, cb, lg, lb, wao, wco, wout)


def _ffn_kernel(x_ref, mod_ref, g_ref, wi_ref, wo_ref, fg_ref, out_ref, *, d_ff, final):
    x = x_ref[0]
    h = _modulated_rms_norm(x, g_ref[...], mod_ref[0, 4:5, :], mod_ref[0, 3:4, :]).astype(BF16)
    acc = jnp.zeros(x.shape, F32)
    for c0 in range(0, d_ff, FFN_CHUNK):
        cw = min(FFN_CHUNK, d_ff - c0)
        f_gate = jnp.dot(h, wi_ref[:, c0:c0 + cw], preferred_element_type=F32)
        f_up = jnp.dot(h, wi_ref[:, d_ff + c0:d_ff + c0 + cw], preferred_element_type=F32)
        a = ((f_gate * _sigmoid(f_gate)) * f_up).astype(BF16)
        acc = acc + jnp.dot(a, wo_ref[c0:c0 + cw, :], preferred_element_type=F32)
    x2 = x + (1.0 + mod_ref[0, 5:6, :]) * acc
    if final:
        x2 = (x2 * lax.rsqrt(jnp.mean(x2 * x2, axis=-1, keepdims=True) + 1e-6)) * fg_ref[...]
    out_ref[0] = x2


def _ffn_call(x, mod, g, wi, wo, final_g, final):
    b, s, d = x.shape
    d_ff = wo.shape[0]
    tm = min(ROW_TILE, s)
    row = lambda bi, i: (bi, i, 0)
    return pl.pallas_call(
        functools.partial(_ffn_kernel, d_ff=d_ff, final=final),
        grid=(b, s // tm),
        in_specs=[
            pl.BlockSpec((1, tm, d), row),
            pl.BlockSpec((1, N_MOD, d), lambda bi, i: (bi, 0, 0)),
            _resident((1, d)),
            _resident(wi.shape),
            _resident(wo.shape),
            _resident((1, d)),
        ],
        out_specs=pl.BlockSpec((1, tm, d), row),
        out_shape=jax.ShapeDtypeStruct((b, s, d), F32),
        compiler_params=pltpu.CompilerParams(
            dimension_semantics=("arbitrary", "arbitrary"),
            vmem_limit_bytes=V7X_VMEM_LIMIT_BYTES),
        name="ffn_final" if final else "ffn",
    )(x, mod, g, wi, wo, final_g)


def _rope_tables(positions):
    half = ROT_DIM // 2
    inv_freq = 1.0 / (ROPE_THETA ** (jnp.arange(0, ROT_DIM, 2, dtype=F32) / ROT_DIM))
    ang = positions.astype(F32)[..., None] * inv_freq
    cos = jnp.cos(ang)
    sin = jnp.sin(ang)
    rest = HEAD_DIM - ROT_DIM
    ones = jnp.ones(cos.shape[:-1] + (rest,), F32)
    zeros_rest = jnp.zeros(cos.shape[:-1] + (rest,), F32)
    zeros_half = jnp.zeros(cos.shape[:-1] + (half,), F32)
    rc = jnp.concatenate([cos, cos, ones], axis=-1)
    rs1 = jnp.concatenate([-sin, zeros_half, zeros_rest], axis=-1)
    rs2 = jnp.concatenate([zeros_half, sin, zeros_rest], axis=-1)
    dup = lambda t: jnp.concatenate([t, t], axis=-1)
    return dup(rc), dup(rs1), dup(rs2)


def kernel(x, c, positions, ada_w, ada_b, norm1_g, w_in, lambda_q1, lambda_k1, lambda_q2, lambda_k2,
           subln_g, w_attn_o, dw_conv_w, dw_conv_b, conv_ln_g, conv_ln_b, w_conv_o, w_out, norm2_g,
           w_ffn_in, w_ffn_out, final_g):
    b, s, d = x.shape
    depth = ada_w.shape[0]
    assert s % min(ROW_TILE, s) == 0 and s % min(ATTN_TILE, s) == 0
    assert ada_w.shape[2] == N_MOD * d and ada_w.shape[2] % ADA_COLS == 0
    assert w_in.shape[2] == N_QKVU + 2 * d

    rows = -(-b // 8) * 8
    c_pad = jnp.zeros((rows, d), F32).at[:b].set(c)
    mod_all = _ada_call(c_pad, ada_w, ada_b)[:, :b].reshape(depth, b, N_MOD, d)

    rc, rs1, rs2 = _rope_tables(positions)
    final_row = final_g.reshape(1, d)

    for l in range(depth):
        lambda_init = 0.8 - 0.6 * math.exp(-0.3 * l)
        mod = mod_all[l]
        g1 = norm1_g[l].reshape(1, d)
        w_l = w_in[l]
        q, k, v, u = _inproj_call(x, mod, g1, w_l[:, :N_QKVU].astype(BF16), rc, rs1, rs2)
        o = _attn_call(q, k, v,
                       lambda_q1[l].reshape(1, HEAD_DIM), lambda_k1[l].reshape(1, HEAD_DIM),
                       lambda_q2[l].reshape(1, HEAD_DIM), lambda_k2[l].reshape(1, HEAD_DIM),
                       subln_g[l].reshape(1, HEAD_W), lambda_init)
        x = _mixer_call(x, mod, g1, w_l[:, N_QKVU:].astype(BF16), o, u,
                        dw_conv_w[l].reshape(CONV_K, CONV_W), dw_conv_b[l].reshape(1, CONV_W),
                        conv_ln_g[l].reshape(1, CONV_W), conv_ln_b[l].reshape(1, CONV_W),
                        w_attn_o[l].astype(BF16), w_conv_o[l].astype(BF16), w_out[l].astype(BF16))
        x = _ffn_call(x, mod, norm2_g[l].reshape(1, d), w_ffn_in[l].astype(BF16),
                      w_ffn_out[l].astype(BF16), final_row, final=(l == depth - 1))
    return x
```

```python
import functools
import math

import jax
import jax.numpy as jnp
from jax import lax
from jax.experimental import pallas as pl
from jax.experimental.pallas import tpu as pltpu

F32 = jnp.float32
BF16 = jnp.bfloat16

ATTN_HEADS = 4
HEAD_DIM = 64
HEAD_W = 2 * HEAD_DIM
ATTN_W = ATTN_HEADS * HEAD_W
ROT_DIM = HEAD_DIM // 4
ROPE_THETA = 500000.0
CONV_W = 512
CONV_K = 31
N_MOD = 6
N_QKVU = 3 * ATTN_W + 2 * CONV_W
CONV_HALO = 32
MASK_VALUE = -1e30

V7X_VMEM_LIMIT_BYTES = 56 * 1024 * 1024
V7X_MXU_COLS = 256

ROW_TILE = 512
ATTN_TILE = 512
ATTN_GROUP = V7X_MXU_COLS
ADA_COLS = 1024
FFN_CHUNK = 1024


def _sigmoid(x):
    return jax.nn.sigmoid(x)


def _modulated_rms_norm(x, g, scale, shift, eps=1e-6):
    y = x * lax.rsqrt(jnp.mean(x * x, axis=-1, keepdims=True) + eps)
    return (y * g) * (1.0 + scale) + shift


def _resident(shape):
    zeros = (0,) * len(shape)
    return pl.BlockSpec(shape, lambda *_: zeros, pipeline_mode=pl.Buffered(1))


def _ada_kernel(c_ref, w_ref, b_ref, o_ref):
    c = c_ref[...]
    c_act = (c * _sigmoid(c)).astype(BF16)
    w = w_ref[0].astype(BF16)
    o_ref[0] = jnp.dot(c_act, w, preferred_element_type=F32) + b_ref[0]


def _ada_call(c_pad, ada_w, ada_b):
    depth, d, n = ada_w.shape
    rows = c_pad.shape[0]
    return pl.pallas_call(
        _ada_kernel,
        grid=(depth, n // ADA_COLS),
        in_specs=[
            pl.BlockSpec((rows, d), lambda l, j: (0, 0)),
            pl.BlockSpec((1, d, ADA_COLS), lambda l, j: (l, 0, j)),
            pl.BlockSpec((1, 1, ADA_COLS), lambda l, j: (l, 0, j)),
        ],
        out_specs=pl.BlockSpec((1, rows, ADA_COLS), lambda l, j: (l, 0, j)),
        out_shape=jax.ShapeDtypeStruct((depth, rows, n), F32),
        compiler_params=pltpu.CompilerParams(
            dimension_semantics=("arbitrary", "arbitrary"),
            vmem_limit_bytes=V7X_VMEM_LIMIT_BYTES),
        name="ada_proj",
    )(c_pad, ada_w, ada_b.reshape(depth, 1, n))


_NT_DIMS = (((1,), (1,)), ((), ()))


def _rope_rows(z, cos, sin):
    half = ROT_DIM // 2
    parts = []
    for comp in range(2):
        r0 = comp * HEAD_DIM
        x1 = z[r0:r0 + half]
        x2 = z[r0 + half:r0 + ROT_DIM]
        parts += [x1 * cos - x2 * sin, x2 * cos + x1 * sin, z[r0 + ROT_DIM:r0 + HEAD_DIM]]
    return jnp.concatenate(parts, axis=0)


def _inproj_kernel(x_ref, mod_ref, g_ref, wt_ref, wc_ref, cos_ref, sin_ref,
                   qt_ref, k_ref, vt_ref, u_ref):
    x = x_ref[0]
    h = _modulated_rms_norm(x, g_ref[...], mod_ref[0, 1:2, :], mod_ref[0, 0:1, :]).astype(BF16)
    cos = cos_ref[0]
    sin = sin_ref[0]

    zq = lax.dot_general(wt_ref[0:ATTN_W, :], h, _NT_DIMS, preferred_element_type=F32)
    for hd in range(ATTN_HEADS):
        t = _rope_rows(zq[hd * HEAD_W:(hd + 1) * HEAD_W], cos, sin)
        qt_ref[0, hd, 0] = (t * (1.0 / math.sqrt(HEAD_DIM))).astype(BF16)

    zk = lax.dot_general(wt_ref[ATTN_W:2 * ATTN_W, :], h, _NT_DIMS, preferred_element_type=F32)
    for hd in range(ATTN_HEADS):
        k_ref[0, hd] = _rope_rows(zk[hd * HEAD_W:(hd + 1) * HEAD_W], cos, sin).T.astype(BF16)

    zv = lax.dot_general(wt_ref[2 * ATTN_W:3 * ATTN_W, :], h, _NT_DIMS, preferred_element_type=F32)
    for hd in range(ATTN_HEADS):
        vt_ref[0, hd, 0] = zv[hd * HEAD_W:(hd + 1) * HEAD_W].astype(BF16)

    ga = jnp.dot(h, wc_ref[:, 0:CONV_W], preferred_element_type=F32)
    gb = jnp.dot(h, wc_ref[:, CONV_W:2 * CONV_W], preferred_element_type=F32)
    u_ref[0] = ga * _sigmoid(gb)


def _inproj_call(x, mod, g, wt, wc, cos_t, sin_t):
    b, s, d = x.shape
    tm = min(ATTN_TILE, s)
    row = lambda bi, i: (bi, i, 0)
    t_spec = pl.BlockSpec((1, ATTN_HEADS, 1, HEAD_W, tm), lambda bi, i: (bi, 0, i, 0, 0))
    t_shape = jax.ShapeDtypeStruct((b, ATTN_HEADS, s // tm, HEAD_W, tm), BF16)
    rot_spec = pl.BlockSpec((1, ROT_DIM // 2, tm), lambda bi, i: (bi, 0, i))
    return pl.pallas_call(
        _inproj_kernel,
        grid=(b, s // tm),
        in_specs=[
            pl.BlockSpec((1, tm, d), row),
            pl.BlockSpec((1, N_MOD, d), lambda bi, i: (bi, 0, 0)),
            _resident((1, d)),
            _resident(wt.shape),
            _resident(wc.shape),
            rot_spec, rot_spec,
        ],
        out_specs=[
            t_spec,
            pl.BlockSpec((1, ATTN_HEADS, tm, HEAD_W), lambda bi, i: (bi, 0, i, 0)),
            t_spec,
            pl.BlockSpec((1, tm, CONV_W), row),
        ],
        out_shape=[
            t_shape,
            jax.ShapeDtypeStruct((b, ATTN_HEADS, s, HEAD_W), BF16),
            t_shape,
            jax.ShapeDtypeStruct((b, s, CONV_W), F32),
        ],
        compiler_params=pltpu.CompilerParams(
            dimension_semantics=("arbitrary", "arbitrary"),
            vmem_limit_bytes=V7X_VMEM_LIMIT_BYTES),
        name="in_proj",
    )(x, mod, g, wt, wc, cos_t, sin_t)


def _attn_kernel(qt_ref, k_ref, vt_ref, lq1_ref, lk1_ref, lq2_ref, lk2_ref, g_ref, o_ref,
                 qbd_scr, m_scr, l_scr, acc_scr, *, tile, lambda_init):
    i = pl.program_id(2)

    qt = qt_ref[0, 0, 0]
    first = lax.broadcasted_iota(jnp.int32, qt.shape, 0) < HEAD_DIM
    zero = jnp.zeros_like(qt)
    qbd_scr[:, 0:tile] = jnp.where(first, qt, zero)
    qbd_scr[:, tile:2 * tile] = jnp.where(first, zero, qt)

    m_scr[...] = jnp.full(m_scr.shape, MASK_VALUE, F32)
    l_scr[...] = jnp.zeros(l_scr.shape, F32)
    acc_scr[...] = jnp.zeros(acc_scr.shape, F32)

    def step(j, masked):
        start = pl.multiple_of(j * tile, tile)
        kb = k_ref[0, 0, pl.ds(start, tile), :]
        vtb = vt_ref[0, 0, j]
        for c0 in range(0, 2 * tile, ATTN_GROUP):
            cols = slice(c0, c0 + ATTN_GROUP)
            s = jnp.dot(kb, qbd_scr[:, cols], preferred_element_type=F32)
            if masked:
                key = lax.broadcasted_iota(jnp.int32, s.shape, 0)
                query = lax.broadcasted_iota(jnp.int32, s.shape, 1) + (c0 % tile)
                s = jnp.where(key <= query, s, MASK_VALUE)
            m_prev = m_scr[:, cols]
            m_new = jnp.maximum(m_prev, jnp.max(s, axis=0, keepdims=True))
            alpha = jnp.exp(m_prev - m_new)
            p = jnp.exp(s - m_new)
            l_scr[:, cols] = alpha * l_scr[:, cols] + jnp.sum(p, axis=0, keepdims=True)
            acc_scr[:, cols] = alpha * acc_scr[:, cols] + jnp.dot(
                vtb, p.astype(BF16), preferred_element_type=F32)
            m_scr[:, cols] = m_new

    def body(j, carry):
        step(j, False)
        return carry

    lax.fori_loop(0, i, body, 0)
    step(i, True)

    lam = (jnp.exp(jnp.sum(lq1_ref[...] * lk1_ref[...], axis=-1, keepdims=True))
           - jnp.exp(jnp.sum(lq2_ref[...] * lk2_ref[...], axis=-1, keepdims=True))
           + lambda_init)
    o1 = acc_scr[:, 0:tile] / l_scr[:, 0:tile]
    o2 = acc_scr[:, tile:2 * tile] / l_scr[:, tile:2 * tile]
    o = o1 - lam * o2
    o = o * lax.rsqrt(jnp.mean(o * o, axis=0, keepdims=True) + 1e-5)
    o = (o * g_ref[...]) * (1.0 - lambda_init)
    o_ref[0] = o.T.astype(BF16)


def _attn_call(qt, k, vt, lq1, lk1, lq2, lk2, subln_g_col, lambda_init):
    b, nh, s, _ = k.shape
    tile = qt.shape[-1]
    vec = lambda n: pl.BlockSpec((1, n), lambda bi, hi, i: (0, 0))
    return pl.pallas_call(
        functools.partial(_attn_kernel, tile=tile, lambda_init=lambda_init),
        grid=(b, nh, s // tile),
        in_specs=[
            pl.BlockSpec((1, 1, 1, HEAD_W, tile), lambda bi, hi, i: (bi, hi, i, 0, 0)),
            pl.BlockSpec((1, 1, s, HEAD_W), lambda bi, hi, i: (bi, hi, 0, 0)),
            pl.BlockSpec((1, 1, s // tile, HEAD_W, tile), lambda bi, hi, i: (bi, hi, 0, 0, 0)),
            vec(HEAD_DIM), vec(HEAD_DIM), vec(HEAD_DIM), vec(HEAD_DIM),
            pl.BlockSpec((HEAD_W, 1), lambda bi, hi, i: (0, 0)),
        ],
        out_specs=pl.BlockSpec((1, tile, HEAD_W), lambda bi, hi, i: (bi, i, hi)),
        out_shape=jax.ShapeDtypeStruct((b, s, nh * HEAD_W), BF16),
        scratch_shapes=[
            pltpu.VMEM((HEAD_W, 2 * tile), BF16),
            pltpu.VMEM((1, 2 * tile), F32),
            pltpu.VMEM((1, 2 * tile), F32),
            pltpu.VMEM((HEAD_W, 2 * tile), F32),
        ],
        compiler_params=pltpu.CompilerParams(
            dimension_semantics=("arbitrary", "arbitrary", "arbitrary"),
            vmem_limit_bytes=V7X_VMEM_LIMIT_BYTES),
        name="diff_attn",
    )(qt, k, vt, lq1, lk1, lq2, lk2, subln_g_col)


def _mixer_kernel(x_ref, mod_ref, g_ref, wg_ref, o_ref, u_ref, uh_ref, cw_ref, cb_ref, lg_ref, lb_ref,
                  wao_ref, wco_ref, wout_ref, x1_ref, ubuf, *, tm):
    i = pl.program_id(1)
    d = x_ref.shape[-1]
    x = x_ref[0]
    h = _modulated_rms_norm(x, g_ref[...], mod_ref[0, 1:2, :], mod_ref[0, 0:1, :]).astype(BF16)
    gates = _sigmoid(jnp.dot(h, wg_ref[...], preferred_element_type=F32))

    halo = uh_ref[0]
    ubuf[0:CONV_HALO, :] = jnp.where(i > 0, halo, jnp.zeros_like(halo))
    ubuf[CONV_HALO:CONV_HALO + tm, :] = u_ref[0]
    base = CONV_HALO - (CONV_K - 1)
    conv = cb_ref[...] + cw_ref[0:1, :] * ubuf[base:base + tm, :]
    for j in range(1, CONV_K):
        conv = conv + cw_ref[j:j + 1, :] * ubuf[base + j:base + j + tm, :]

    mu = jnp.mean(conv, axis=-1, keepdims=True)
    cen = conv - mu
    var = jnp.mean(cen * cen, axis=-1, keepdims=True)
    y = (cen * lax.rsqrt(var + 1e-5)) * lg_ref[...] + lb_ref[...]
    act = (y * _sigmoid(y)).astype(BF16)

    y_conv = jnp.dot(act, wco_ref[...], preferred_element_type=F32)
    y_attn = jnp.dot(o_ref[0], wao_ref[...], preferred_element_type=F32)
    mix = (gates[:, 0:d] * y_attn + gates[:, d:2 * d] * y_conv).astype(BF16)
    mixed = jnp.dot(mix, wout_ref[...], preferred_element_type=F32)
    x1_ref[0] = x + (1.0 + mod_ref[0, 2:3, :]) * mixed


def _mixer_call(x, mod, g, wg, o, u, cw, cb, lg, lb, wao, wco, wout):
    b, s, d = x.shape
    tm = min(ROW_TILE, s)
    halo_blocks = tm // CONV_HALO
    row = lambda bi, i: (bi, i, 0)
    return pl.pallas_call(
        functools.partial(_mixer_kernel, tm=tm),
        grid=(b, s // tm),
        in_specs=[
            pl.BlockSpec((1, tm, d), row),
            pl.BlockSpec((1, N_MOD, d), lambda bi, i: (bi, 0, 0)),
            _resident((1, d)),
            _resident(wg.shape),
            pl.BlockSpec((1, tm, ATTN_W), row),
            pl.BlockSpec((1, tm, CONV_W), row),
            pl.BlockSpec((1, CONV_HALO, CONV_W), lambda bi, i: (bi, jnp.maximum(i * halo_blocks - 1, 0), 0)),
            _resident(cw.shape), _resident(cb.shape), _resident(lg.shape), _resident(lb.shape),
            _resident(wao.shape), _resident(wco.shape), _resident(wout.shape),
        ],
        out_specs=pl.BlockSpec((1, tm, d), row),
        out_shape=jax.ShapeDtypeStruct((b, s, d), F32),
        scratch_shapes=[pltpu.VMEM((CONV_HALO + tm, CONV_W), F32)],
        compiler_params=pltpu.CompilerParams(
            dimension_semantics=("arbitrary", "arbitrary"),
            vmem_limit_bytes=V7X_VMEM_LIMIT_BYTES),
        name="mixer",
    )(x, mod, g, wg, o, u, u, cw, cb, lg, lb, wao, wco, wout)


def _ffn_kernel(x_ref, mod_ref, g_ref, wi_ref, wo_ref, fg_ref, out_ref, *, d_ff, final):
    x = x_ref[0]
    h = _modulated_rms_norm(x, g_ref[...], mod_ref[0, 4:5, :], mod_ref[0, 3:4, :]).astype(BF16)
    acc = jnp.zeros(x.shape, F32)
    for c0 in range(0, d_ff, FFN_CHUNK):
        cw = min(FFN_CHUNK, d_ff - c0)
        f_gate = jnp.dot(h, wi_ref[:, c0:c0 + cw], preferred_element_type=F32)
        f_up = jnp.dot(h, wi_ref[:, d_ff + c0:d_ff + c0 + cw], preferred_element_type=F32)
        a = ((f_gate * _sigmoid(f_gate)) * f_up).astype(BF16)
        acc = acc + jnp.dot(a, wo_ref[c0:c0 + cw, :], preferred_element_type=F32)
    x2 = x + (1.0 + mod_ref[0, 5:6, :]) * acc
    if final:
        x2 = (x2 * lax.rsqrt(jnp.mean(x2 * x2, axis=-1, keepdims=True) + 1e-6)) * fg_ref[...]
    out_ref[0] = x2


def _ffn_call(x, mod, g, wi, wo, final_g, final):
    b, s, d = x.shape
    d_ff = wo.shape[0]
    tm = min(ROW_TILE, s)
    row = lambda bi, i: (bi, i, 0)
    return pl.pallas_call(
        functools.partial(_ffn_kernel, d_ff=d_ff, final=final),
        grid=(b, s // tm),
        in_specs=[
            pl.BlockSpec((1, tm, d), row),
            pl.BlockSpec((1, N_MOD, d), lambda bi, i: (bi, 0, 0)),
            _resident((1, d)),
            _resident(wi.shape),
            _resident(wo.shape),
            _resident((1, d)),
        ],
        out_specs=pl.BlockSpec((1, tm, d), row),
        out_shape=jax.ShapeDtypeStruct((b, s, d), F32),
        compiler_params=pltpu.CompilerParams(
            dimension_semantics=("arbitrary", "arbitrary"),
            vmem_limit_bytes=V7X_VMEM_LIMIT_BYTES),
        name="ffn_final" if final else "ffn",
    )(x, mod, g, wi, wo, final_g)


def _rope_tables(positions):
    inv_freq = 1.0 / (ROPE_THETA ** (jnp.arange(0, ROT_DIM, 2, dtype=F32) / ROT_DIM))
    ang = positions.astype(F32)[..., None] * inv_freq
    return jnp.swapaxes(jnp.cos(ang), 1, 2), jnp.swapaxes(jnp.sin(ang), 1, 2)


def kernel(x, c, positions, ada_w, ada_b, norm1_g, w_in, lambda_q1, lambda_k1, lambda_q2, lambda_k2,
           subln_g, w_attn_o, dw_conv_w, dw_conv_b, conv_ln_g, conv_ln_b, w_conv_o, w_out, norm2_g,
           w_ffn_in, w_ffn_out, final_g):
    b, s, d = x.shape
    depth = ada_w.shape[0]
    assert s % min(ROW_TILE, s) == 0 and s % min(ATTN_TILE, s) == 0
    assert (2 * min(ATTN_TILE, s)) % ATTN_GROUP == 0
    assert ada_w.shape[2] == N_MOD * d and ada_w.shape[2] % ADA_COLS == 0
    assert w_in.shape[2] == N_QKVU + 2 * d

    rows = -(-b // 8) * 8
    c_pad = jnp.zeros((rows, d), F32).at[:b].set(c)
    mod_all = _ada_call(c_pad, ada_w, ada_b)[:, :b].reshape(depth, b, N_MOD, d)

    cos_t, sin_t = _rope_tables(positions)
    final_row = final_g.reshape(1, d)

    for l in range(depth):
        lambda_init = 0.8 - 0.6 * math.exp(-0.3 * l)
        mod = mod_all[l]
        g1 = norm1_g[l].reshape(1, d)
        w_l = w_in[l]
        qt, k, vt, u = _inproj_call(x, mod, g1, w_l[:, :3 * ATTN_W].T.astype(BF16),
                                    w_l[:, 3 * ATTN_W:N_QKVU].astype(BF16), cos_t, sin_t)
        o = _attn_call(qt, k, vt,
                       lambda_q1[l].reshape(1, HEAD_DIM), lambda_k1[l].reshape(1, HEAD_DIM),
                       lambda_q2[l].reshape(1, HEAD_DIM), lambda_k2[l].reshape(1, HEAD_DIM),
                       subln_g[l].reshape(HEAD_W, 1), lambda_init)
        x = _mixer_call(x, mod, g1, w_l[:, N_QKVU:].astype(BF16), o, u,
                        dw_conv_w[l].reshape(CONV_K, CONV_W), dw_conv_b[l].reshape(1, CONV_W),
                        conv_ln_g[l].reshape(1, CONV_W), conv_ln_b[l].reshape(1, CONV_W),
                        w_attn_o[l].astype(BF16), w_conv_o[l].astype(BF16), w_out[l].astype(BF16))
        x = _ffn_call(x, mod, norm2_g[l].reshape(1, d), w_ffn_in[l].astype(BF16),
                      w_ffn_out[l].astype(BF16), final_row, final=(l == depth - 1))
    return x
```

```python
import functools
import math

import jax
import jax.numpy as jnp
from jax import lax
from jax.experimental import pallas as pl
from jax.experimental.pallas import tpu as pltpu

F32 = jnp.float32
BF16 = jnp.bfloat16

ATTN_HEADS = 4
HEAD_DIM = 64
HEAD_W = 2 * HEAD_DIM
ATTN_W = ATTN_HEADS * HEAD_W
ROT_DIM = HEAD_DIM // 4
ROPE_THETA = 500000.0
CONV_W = 512
CONV_K = 31
N_MOD = 6
N_QKVU = 3 * ATTN_W + 2 * CONV_W
CONV_HALO = 32
MASK_VALUE = -1e30

V7X_VMEM_LIMIT_BYTES = 56 * 1024 * 1024
V7X_MXU_COLS = 256

ROW_TILE = 512
ATTN_TILE = 512
ATTN_GROUP = V7X_MXU_COLS
ADA_COLS = 1024
FFN_CHUNK = 1024


def _sigmoid(x):
    return jax.nn.sigmoid(x)


def _modulated_rms_norm(x, g, scale, shift, eps=1e-6):
    y = x * lax.rsqrt(jnp.mean(x * x, axis=-1, keepdims=True) + eps)
    return (y * g) * (1.0 + scale) + shift


def _resident(shape):
    zeros = (0,) * len(shape)
    return pl.BlockSpec(shape, lambda *_: zeros, pipeline_mode=pl.Buffered(1))


def _ada_kernel(c_ref, w_ref, b_ref, o_ref):
    c = c_ref[...]
    c_act = (c * _sigmoid(c)).astype(BF16)
    w = w_ref[0].astype(BF16)
    o_ref[0] = jnp.dot(c_act, w, preferred_element_type=F32) + b_ref[0]


def _ada_call(c_pad, ada_w, ada_b):
    depth, d, n = ada_w.shape
    rows = c_pad.shape[0]
    return pl.pallas_call(
        _ada_kernel,
        grid=(depth, n // ADA_COLS),
        in_specs=[
            pl.BlockSpec((rows, d), lambda l, j: (0, 0)),
            pl.BlockSpec((1, d, ADA_COLS), lambda l, j: (l, 0, j)),
            pl.BlockSpec((1, 1, ADA_COLS), lambda l, j: (l, 0, j)),
        ],
        out_specs=pl.BlockSpec((1, rows, ADA_COLS), lambda l, j: (l, 0, j)),
        out_shape=jax.ShapeDtypeStruct((depth, rows, n), F32),
        compiler_params=pltpu.CompilerParams(
            dimension_semantics=("arbitrary", "arbitrary"),
            vmem_limit_bytes=V7X_VMEM_LIMIT_BYTES),
        name="ada_proj",
    )(c_pad, ada_w, ada_b.reshape(depth, 1, n))


_NT_DIMS = (((1,), (1,)), ((), ()))


def _rope_rows(z, cos, sin):
    half = ROT_DIM // 2
    parts = []
    for comp in range(2):
        r0 = comp * HEAD_DIM
        x1 = z[r0:r0 + half]
        x2 = z[r0 + half:r0 + ROT_DIM]
        parts += [x1 * cos - x2 * sin, x2 * cos + x1 * sin, z[r0 + ROT_DIM:r0 + HEAD_DIM]]
    return jnp.concatenate(parts, axis=0)


def _inproj_kernel(x_ref, mod_ref, g_ref, wt_ref, wc_ref, cos_ref, sin_ref,
                   qt_ref, k_ref, vt_ref, u_ref):
    x = x_ref[0]
    h = _modulated_rms_norm(x, g_ref[...], mod_ref[0, 1:2, :], mod_ref[0, 0:1, :]).astype(BF16)
    cos = cos_ref[0]
    sin = sin_ref[0]

    zq = lax.dot_general(wt_ref[0:ATTN_W, :], h, _NT_DIMS, preferred_element_type=F32)
    for hd in range(ATTN_HEADS):
        t = _rope_rows(zq[hd * HEAD_W:(hd + 1) * HEAD_W], cos, sin)
        qt_ref[0, hd, 0] = (t * (1.0 / math.sqrt(HEAD_DIM))).astype(BF16)

    zk = lax.dot_general(wt_ref[ATTN_W:2 * ATTN_W, :], h, _NT_DIMS, preferred_element_type=F32)
    for hd in range(ATTN_HEADS):
        k_ref[0, hd] = _rope_rows(zk[hd * HEAD_W:(hd + 1) * HEAD_W], cos, sin).T.astype(BF16)

    zv = lax.dot_general(wt_ref[2 * ATTN_W:3 * ATTN_W, :], h, _NT_DIMS, preferred_element_type=F32)
    for hd in range(ATTN_HEADS):
        vt_ref[0, hd, 0] = zv[hd * HEAD_W:(hd + 1) * HEAD_W].astype(BF16)

    ga = jnp.dot(h, wc_ref[:, 0:CONV_W], preferred_element_type=F32)
    gb = jnp.dot(h, wc_ref[:, CONV_W:2 * CONV_W], preferred_element_type=F32)
    u_ref[0] = ga * _sigmoid(gb)


def _inproj_call(x, mod, g, wt, wc, cos_t, sin_t):
    b, s, d = x.shape
    tm = min(ATTN_TILE, s)
    row = lambda bi, i: (bi, i, 0)
    t_spec = pl.BlockSpec((1, ATTN_HEADS, 1, HEAD_W, tm), lambda bi, i: (bi, 0, i, 0, 0))
    t_shape = jax.ShapeDtypeStruct((b, ATTN_HEADS, s // tm, HEAD_W, tm), BF16)
    rot_spec = pl.BlockSpec((1, ROT_DIM // 2, tm), lambda bi, i: (bi, 0, i))
    return pl.pallas_call(
        _inproj_kernel,
        grid=(b, s // tm),
        in_specs=[
            pl.BlockSpec((1, tm, d), row),
            pl.BlockSpec((1, N_MOD, d), lambda bi, i: (bi, 0, 0)),
            _resident((1, d)),
            _resident(wt.shape),
            _resident(wc.shape),
            rot_spec, rot_spec,
        ],
        out_specs=[
            t_spec,
            pl.BlockSpec((1, ATTN_HEADS, tm, HEAD_W), lambda bi, i: (bi, 0, i, 0)),
            t_spec,
            pl.BlockSpec((1, tm, CONV_W), row),
        ],
        out_shape=[
            t_shape,
            jax.ShapeDtypeStruct((b, ATTN_HEADS, s, HEAD_W), BF16),
            t_shape,
            jax.ShapeDtypeStruct((b, s, CONV_W), F32),
        ],
        compiler_params=pltpu.CompilerParams(
            dimension_semantics=("arbitrary", "arbitrary"),
            vmem_limit_bytes=V7X_VMEM_LIMIT_BYTES),
        name="in_proj",
    )(x, mod, g, wt, wc, cos_t, sin_t)


def _attn_kernel(qt_ref, k_ref, vt_ref, lq1_ref, lk1_ref, lq2_ref, lk2_ref, g_ref, o_ref,
                 qbd_scr, m_scr, l_scr, acc_scr, s0, s1, p0, p1, a0, a1, *, tile, lambda_init):
    i = pl.program_id(2)
    groups = [slice(c0, c0 + ATTN_GROUP) for c0 in range(0, 2 * tile, ATTN_GROUP)]

    qt = qt_ref[0, 0, 0]
    first = lax.broadcasted_iota(jnp.int32, qt.shape, 0) < HEAD_DIM
    zero = jnp.zeros_like(qt)
    qbd_scr[:, 0:tile] = jnp.where(first, qt, zero)
    qbd_scr[:, tile:2 * tile] = jnp.where(first, zero, qt)

    m_scr[...] = jnp.full(m_scr.shape, MASK_VALUE, F32)
    l_scr[...] = jnp.zeros(l_scr.shape, F32)
    acc_scr[...] = jnp.zeros(acc_scr.shape, F32)
    p1[...] = jnp.zeros(p1.shape, BF16)
    a1[...] = jnp.ones(a1.shape, F32)

    def scores(j, s_ref, cols):
        start = pl.multiple_of(j * tile, tile)
        kb = k_ref[0, 0, pl.ds(start, tile), :]
        s_ref[:, cols] = jnp.dot(kb, qbd_scr[:, cols], preferred_element_type=F32)

    def softmax(s_ref, p_ref, a_ref, cols, masked):
        s = s_ref[:, cols]
        if masked:
            key = lax.broadcasted_iota(jnp.int32, s.shape, 0)
            query = lax.broadcasted_iota(jnp.int32, s.shape, 1) + (cols.start % tile)
            s = jnp.where(key <= query, s, MASK_VALUE)
        m_prev = m_scr[:, cols]
        m_new = jnp.maximum(m_prev, jnp.max(s, axis=0, keepdims=True))
        alpha = jnp.exp(m_prev - m_new)
        p = jnp.exp(s - m_new)
        l_scr[:, cols] = alpha * l_scr[:, cols] + jnp.sum(p, axis=0, keepdims=True)
        p_ref[:, cols] = p.astype(BF16)
        a_ref[:, cols] = alpha
        m_scr[:, cols] = m_new

    def accumulate(j, p_ref, a_ref, cols):
        acc_scr[:, cols] = a_ref[:, cols] * acc_scr[:, cols] + jnp.dot(
            vt_ref[0, 0, j], p_ref[:, cols], preferred_element_type=F32)

    even = (s0, p0, a0)
    odd = (s1, p1, a1)

    def substep(j, cur, other):
        j_prev = jnp.maximum(j - 1, 0)
        for cols in groups:
            scores(j + 1, other[0], cols)
            softmax(cur[0], cur[1], cur[2], cols, False)
            accumulate(j_prev, other[1], other[2], cols)

    def last_block(cur, other):
        j_prev = jnp.maximum(i - 1, 0)
        for cols in groups:
            softmax(cur[0], cur[1], cur[2], cols, True)
            accumulate(j_prev, other[1], other[2], cols)
        for cols in groups:
            accumulate(i, cur[1], cur[2], cols)

    for cols in groups:
        scores(0, s0, cols)

    def pair(jj, carry):
        substep(2 * jj, even, odd)
        substep(2 * jj + 1, odd, even)
        return carry

    lax.fori_loop(0, lax.shift_right_logical(i, 1), pair, 0)

    @pl.when((i & 1) == 1)
    def _():
        substep(i - 1, even, odd)
        last_block(odd, even)

    @pl.when((i & 1) == 0)
    def _():
        last_block(even, odd)

    lam = (jnp.exp(jnp.sum(lq1_ref[...] * lk1_ref[...], axis=-1, keepdims=True))
           - jnp.exp(jnp.sum(lq2_ref[...] * lk2_ref[...], axis=-1, keepdims=True))
           + lambda_init)
    o1 = acc_scr[:, 0:tile] / l_scr[:, 0:tile]
    o2 = acc_scr[:, tile:2 * tile] / l_scr[:, tile:2 * tile]
    o = o1 - lam * o2
    o = o * lax.rsqrt(jnp.mean(o * o, axis=0, keepdims=True) + 1e-5)
    o = (o * g_ref[...]) * (1.0 - lambda_init)
    o_ref[0] = o.T.astype(BF16)


def _attn_call(qt, k, vt, lq1, lk1, lq2, lk2, subln_g_col, lambda_init):
    b, nh, s, _ = k.shape
    tile = qt.shape[-1]
    vec = lambda n: pl.BlockSpec((1, n), lambda bi, hi, i: (0, 0))
    return pl.pallas_call(
        functools.partial(_attn_kernel, tile=tile, lambda_init=lambda_init),
        grid=(b, nh, s // tile),
        in_specs=[
            pl.BlockSpec((1, 1, 1, HEAD_W, tile), lambda bi, hi, i: (bi, hi, i, 0, 0)),
            pl.BlockSpec((1, 1, s, HEAD_W), lambda bi, hi, i: (bi, hi, 0, 0)),
            pl.BlockSpec((1, 1, s // tile, HEAD_W, tile), lambda bi, hi, i: (bi, hi, 0, 0, 0)),
            vec(HEAD_DIM), vec(HEAD_DIM), vec(HEAD_DIM), vec(HEAD_DIM),
            pl.BlockSpec((HEAD_W, 1), lambda bi, hi, i: (0, 0)),
        ],
        out_specs=pl.BlockSpec((1, tile, HEAD_W), lambda bi, hi, i: (bi, i, hi)),
        out_shape=jax.ShapeDtypeStruct((b, s, nh * HEAD_W), BF16),
        scratch_shapes=[
            pltpu.VMEM((HEAD_W, 2 * tile), BF16),
            pltpu.VMEM((1, 2 * tile), F32),
            pltpu.VMEM((1, 2 * tile), F32),
            pltpu.VMEM((HEAD_W, 2 * tile), F32),
            pltpu.VMEM((tile, 2 * tile), F32), pltpu.VMEM((tile, 2 * tile), F32),
            pltpu.VMEM((tile, 2 * tile), BF16), pltpu.VMEM((tile, 2 * tile), BF16),
            pltpu.VMEM((1, 2 * tile), F32), pltpu.VMEM((1, 2 * tile), F32),
        ],
        compiler_params=pltpu.CompilerParams(
            dimension_semantics=("arbitrary", "arbitrary", "arbitrary"),
            vmem_limit_bytes=V7X_VMEM_LIMIT_BYTES),
        name="diff_attn",
    )(qt, k, vt, lq1, lk1, lq2, lk2, subln_g_col)


def _mixer_kernel(x_ref, mod_ref, g_ref, wg_ref, o_ref, u_ref, uh_ref, cw_ref, cb_ref, lg_ref, lb_ref,
                  wao_ref, wco_ref, wout_ref, x1_ref, ubuf, *, tm):
    i = pl.program_id(1)
    d = x_ref.shape[-1]
    x = x_ref[0]
    h = _modulated_rms_norm(x, g_ref[...], mod_ref[0, 1:2, :], mod_ref[0, 0:1, :]).astype(BF16)
    gates = _sigmoid(jnp.dot(h, wg_ref[...], preferred_element_type=F32))

    halo = uh_ref[0]
    ubuf[0:CONV_HALO, :] = jnp.where(i > 0, halo, jnp.zeros_like(halo))
    ubuf[CONV_HALO:CONV_HALO + tm, :] = u_ref[0]
    base = CONV_HALO - (CONV_K - 1)
    conv = cb_ref[...] + cw_ref[0:1, :] * ubuf[base:base + tm, :]
    for j in range(1, CONV_K):
        conv = conv + cw_ref[j:j + 1, :] * ubuf[base + j:base + j + tm, :]

    mu = jnp.mean(conv, axis=-1, keepdims=True)
    cen = conv - mu
    var = jnp.mean(cen * cen, axis=-1, keepdims=True)
    y = (cen * lax.rsqrt(var + 1e-5)) * lg_ref[...] + lb_ref[...]
    act = (y * _sigmoid(y)).astype(BF16)

    y_conv = jnp.dot(act, wco_ref[...], preferred_element_type=F32)
    y_attn = jnp.dot(o_ref[0], wao_ref[...], preferred_element_type=F32)
    mix = (gates[:, 0:d] * y_attn + gates[:, d:2 * d] * y_conv).astype(BF16)
    mixed = jnp.dot(mix, wout_ref[...], preferred_element_type=F32)
    x1_ref[0] = x + (1.0 + mod_ref[0, 2:3, :]) * mixed


def _mixer_call(x, mod, g, wg, o, u, cw, cb, lg, lb, wao, wco, wout):
    b, s, d = x.shape
    tm = min(ROW_TILE, s)
    halo_blocks = tm // CONV_HALO
    row = lambda bi, i: (bi, i, 0)
    return pl.pallas_call(
        functools.partial(_mixer_kernel, tm=tm),
        grid=(b, s // tm),
        in_specs=[
            pl.BlockSpec((1, tm, d), row),
            pl.BlockSpec((1, N_MOD, d), lambda bi, i: (bi, 0, 0)),
            _resident((1, d)),
            _resident(wg.shape),
            pl.BlockSpec((1, tm, ATTN_W), row),
            pl.BlockSpec((1, tm, CONV_W), row),
            pl.BlockSpec((1, CONV_HALO, CONV_W), lambda bi, i: (bi, jnp.maximum(i * halo_blocks - 1, 0), 0)),
            _resident(cw.shape), _resident(cb.shape), _resident(lg.shape), _resident(lb.shape),
            _resident(wao.shape), _resident(wco.shape), _resident(wout.shape),
        ],
        out_specs=pl.BlockSpec((1, tm, d), row),
        out_shape=jax.ShapeDtypeStruct((b, s, d), F32),
        scratch_shapes=[pltpu.VMEM((CONV_HALO + tm, CONV_W), F32)],
        compiler_params=pltpu.CompilerParams(
            dimension_semantics=("arbitrary", "arbitrary"),
            vmem_limit_bytes=V7X_VMEM_LIMIT_BYTES),
        name="mixer",
    )(x, mod, g, wg, o, u, u, cw, cb, lg, lb, wao, wco, wout)


def _ffn_kernel(x_ref, mod_ref, g_ref, wi_ref, wo_ref, fg_ref, out_ref, *, d_ff, final):
    x = x_ref[0]
    h = _modulated_rms_norm(x, g_ref[...], mod_ref[0, 4:5, :], mod_ref[0, 3:4, :]).astype(BF16)
    acc = jnp.zeros(x.shape, F32)
    for c0 in range(0, d_ff, FFN_CHUNK):
        cw = min(FFN_CHUNK, d_ff - c0)
        f_gate = jnp.dot(h, wi_ref[:, c0:c0 + cw], preferred_element_type=F32)
        f_up = jnp.dot(h, wi_ref[:, d_ff + c0:d_ff + c0 + cw], preferred_element_type=F32)
        a = ((f_gate * _sigmoid(f_gate)) * f_up).astype(BF16)
        acc = acc + jnp.dot(a, wo_ref[c0:c0 + cw, :], preferred_element_type=F32)
    x2 = x + (1.0 + mod_ref[0, 5:6, :]) * acc
    if final:
        x2 = (x2 * lax.rsqrt(jnp.mean(x2 * x2, axis=-1, keepdims=True) + 1e-6)) * fg_ref[...]
    out_ref[0] = x2


def _ffn_call(x, mod, g, wi, wo, final_g, final):
    b, s, d = x.shape
    d_ff = wo.shape[0]
    tm = min(ROW_TILE, s)
    row = lambda bi, i: (bi, i, 0)
    return pl.pallas_call(
        functools.partial(_ffn_kernel, d_ff=d_ff, final=final),
        grid=(b, s // tm),
        in_specs=[
            pl.BlockSpec((1, tm, d), row),
            pl.BlockSpec((1, N_MOD, d), lambda bi, i: (bi, 0, 0)),
            _resident((1, d)),
            _resident(wi.shape),
            _resident(wo.shape),
            _resident((1, d)),
        ],
        out_specs=pl.BlockSpec((1, tm, d), row),
        out_shape=jax.ShapeDtypeStruct((b, s, d), F32),
        compiler_params=pltpu.CompilerParams(
            dimension_semantics=("arbitrary", "arbitrary"),
            vmem_limit_bytes=V7X_VMEM_LIMIT_BYTES),
        name="ffn_final" if final else "ffn",
    )(x, mod, g, wi, wo, final_g)


def _rope_tables(positions):
    inv_freq = 1.0 / (ROPE_THETA ** (jnp.arange(0, ROT_DIM, 2, dtype=F32) / ROT_DIM))
    ang = positions.astype(F32)[..., None] * inv_freq
    return jnp.swapaxes(jnp.cos(ang), 1, 2), jnp.swapaxes(jnp.sin(ang), 1, 2)


def kernel(x, c, positions, ada_w, ada_b, norm1_g, w_in, lambda_q1, lambda_k1, lambda_q2, lambda_k2,
           subln_g, w_attn_o, dw_conv_w, dw_conv_b, conv_ln_g, conv_ln_b, w_conv_o, w_out, norm2_g,
           w_ffn_in, w_ffn_out, final_g):
    b, s, d = x.shape
    depth = ada_w.shape[0]
    assert s % min(ROW_TILE, s) == 0 and s % min(ATTN_TILE, s) == 0
    assert (2 * min(ATTN_TILE, s)) % ATTN_GROUP == 0
    assert ada_w.shape[2] == N_MOD * d and ada_w.shape[2] % ADA_COLS == 0
    assert w_in.shape[2] == N_QKVU + 2 * d

    rows = -(-b // 8) * 8
    c_pad = jnp.zeros((rows, d), F32).at[:b].set(c)
    mod_all = _ada_call(c_pad, ada_w, ada_b)[:, :b].reshape(depth, b, N_MOD, d)

    cos_t, sin_t = _rope_tables(positions)
    final_row = final_g.reshape(1, d)

    for l in range(depth):
        lambda_init = 0.8 - 0.6 * math.exp(-0.3 * l)
        mod = mod_all[l]
        g1 = norm1_g[l].reshape(1, d)
        w_l = w_in[l]
        qt, k, vt, u = _inproj_call(x, mod, g1, w_l[:, :3 * ATTN_W].T.astype(BF16),
                                    w_l[:, 3 * ATTN_W:N_QKVU].astype(BF16), cos_t, sin_t)
        o = _attn_call(qt, k, vt,
                       lambda_q1[l].reshape(1, HEAD_DIM), lambda_k1[l].reshape(1, HEAD_DIM),
                       lambda_q2[l].reshape(1, HEAD_DIM), lambda_k2[l].reshape(1, HEAD_DIM),
                       subln_g[l].reshape(HEAD_W, 1), lambda_init)
        x = _mixer_call(x, mod, g1, w_l[:, N_QKVU:].astype(BF16), o, u,
                        dw_conv_w[l].reshape(CONV_K, CONV_W), dw_conv_b[l].reshape(1, CONV_W),
                        conv_ln_g[l].reshape(1, CONV_W), conv_ln_b[l].reshape(1, CONV_W),
                        w_attn_o[l].astype(BF16), w_conv_o[l].astype(BF16), w_out[l].astype(BF16))
        x = _ffn_call(x, mod, norm2_g[l].reshape(1, d), w_ffn_in[l].astype(BF16),
                      w_ffn_out[l].astype(BF16), final_row, final=(l == depth - 1))
    return x
```

```python
import functools
import math

import jax
import jax.numpy as jnp
from jax import lax
from jax.experimental import pallas as pl
from jax.experimental.pallas import tpu as pltpu

F32 = jnp.float32
BF16 = jnp.bfloat16

ATTN_HEADS = 4
HEAD_DIM = 64
HEAD_W = 2 * HEAD_DIM
ATTN_W = ATTN_HEADS * HEAD_W
ROT_DIM = HEAD_DIM // 4
ROPE_THETA = 500000.0
CONV_W = 512
CONV_K = 31
N_MOD = 6
N_QKVU = 3 * ATTN_W + 2 * CONV_W
CONV_HALO = 32
MASK_VALUE = -1e30
Q_SCALE = math.log2(math.e) / math.sqrt(HEAD_DIM)
V_ROWS = HEAD_W + 16

V7X_VMEM_LIMIT_BYTES = 56 * 1024 * 1024
V7X_MXU_COLS = 256
SUBLANES = 8
LANES = 128

ROW_TILE = 512
ATTN_TILE = 512
ATTN_GROUP = V7X_MXU_COLS
ADA_COLS = 1024
FFN_CHUNK = 1024


def _sigmoid(x):
    return jax.nn.sigmoid(x)


def _modulated_rms_norm(x, g, scale, shift, eps=1e-6):
    y = x * lax.rsqrt(jnp.mean(x * x, axis=-1, keepdims=True) + eps)
    return (y * g) * (1.0 + scale) + shift


def _resident(shape):
    zeros = (0,) * len(shape)
    return pl.BlockSpec(shape, lambda *_: zeros, pipeline_mode=pl.Buffered(1))


def _ada_kernel(c_ref, w_ref, b_ref, o_ref):
    c = c_ref[...]
    c_act = (c * _sigmoid(c)).astype(BF16)
    w = w_ref[0].astype(BF16)
    o_ref[0] = jnp.dot(c_act, w, preferred_element_type=F32) + b_ref[0]


def _ada_call(c_pad, ada_w, ada_b):
    depth, d, n = ada_w.shape
    rows = c_pad.shape[0]
    return pl.pallas_call(
        _ada_kernel,
        grid=(depth, n // ADA_COLS),
        in_specs=[
            pl.BlockSpec((rows, d), lambda l, j: (0, 0)),
            pl.BlockSpec((1, d, ADA_COLS), lambda l, j: (l, 0, j)),
            pl.BlockSpec((1, 1, ADA_COLS), lambda l, j: (l, 0, j)),
        ],
        out_specs=pl.BlockSpec((1, rows, ADA_COLS), lambda l, j: (l, 0, j)),
        out_shape=jax.ShapeDtypeStruct((depth, rows, n), F32),
        compiler_params=pltpu.CompilerParams(
            dimension_semantics=("arbitrary", "arbitrary"),
            vmem_limit_bytes=V7X_VMEM_LIMIT_BYTES),
        name="ada_proj",
    )(c_pad, ada_w, ada_b.reshape(depth, 1, n))


_NT_DIMS = (((1,), (1,)), ((), ()))


def _rope_rows(z, cos, sin):
    half = ROT_DIM // 2
    parts = []
    for comp in range(2):
        r0 = comp * HEAD_DIM
        x1 = z[r0:r0 + half]
        x2 = z[r0 + half:r0 + ROT_DIM]
        parts += [x1 * cos - x2 * sin, x2 * cos + x1 * sin, z[r0 + ROT_DIM:r0 + HEAD_DIM]]
    return jnp.concatenate(parts, axis=0)


def _inproj_kernel(x_ref, mod_ref, g_ref, wt_ref, wc_ref, cos_ref, sin_ref,
                   qt_ref, k_ref, vt_ref, u_ref):
    x = x_ref[0]
    h = _modulated_rms_norm(x, g_ref[...], mod_ref[0, 1:2, :], mod_ref[0, 0:1, :]).astype(BF16)
    cos = cos_ref[0]
    sin = sin_ref[0]

    zq = lax.dot_general(wt_ref[0:ATTN_W, :], h, _NT_DIMS, preferred_element_type=F32)
    for hd in range(ATTN_HEADS):
        t = _rope_rows(zq[hd * HEAD_W:(hd + 1) * HEAD_W], cos, sin)
        qt_ref[0, hd, 0] = (t * Q_SCALE).astype(BF16)

    zk = lax.dot_general(wt_ref[ATTN_W:2 * ATTN_W, :], h, _NT_DIMS, preferred_element_type=F32)
    for hd in range(ATTN_HEADS):
        k_ref[0, hd] = _rope_rows(zk[hd * HEAD_W:(hd + 1) * HEAD_W], cos, sin).T.astype(BF16)

    zv = lax.dot_general(wt_ref[2 * ATTN_W:3 * ATTN_W, :], h, _NT_DIMS, preferred_element_type=F32)
    for hd in range(ATTN_HEADS):
        vt_ref[0, hd, 0, 0:HEAD_W, :] = zv[hd * HEAD_W:(hd + 1) * HEAD_W].astype(BF16)
        vt_ref[0, hd, 0, HEAD_W:V_ROWS, :] = jnp.ones((V_ROWS - HEAD_W, zv.shape[1]), BF16)

    ga = jnp.dot(h, wc_ref[:, 0:CONV_W], preferred_element_type=F32)
    gb = jnp.dot(h, wc_ref[:, CONV_W:2 * CONV_W], preferred_element_type=F32)
    u_ref[0] = ga * _sigmoid(gb)


def _inproj_call(x, mod, g, wt, wc, cos_t, sin_t):
    b, s, d = x.shape
    tm = min(ATTN_TILE, s)
    row = lambda bi, i: (bi, i, 0)
    t_spec = lambda rows: pl.BlockSpec((1, ATTN_HEADS, 1, rows, tm), lambda bi, i: (bi, 0, i, 0, 0))
    t_shape = lambda rows: jax.ShapeDtypeStruct((b, ATTN_HEADS, s // tm, rows, tm), BF16)
    rot_spec = pl.BlockSpec((1, ROT_DIM // 2, tm), lambda bi, i: (bi, 0, i))
    return pl.pallas_call(
        _inproj_kernel,
        grid=(b, s // tm),
        in_specs=[
            pl.BlockSpec((1, tm, d), row),
            pl.BlockSpec((1, N_MOD, d), lambda bi, i: (bi, 0, 0)),
            _resident((1, d)),
            _resident(wt.shape),
            _resident(wc.shape),
            rot_spec, rot_spec,
        ],
        out_specs=[
            t_spec(HEAD_W),
            pl.BlockSpec((1, ATTN_HEADS, tm, HEAD_W), lambda bi, i: (bi, 0, i, 0)),
            t_spec(V_ROWS),
            pl.BlockSpec((1, tm, CONV_W), row),
        ],
        out_shape=[
            t_shape(HEAD_W),
            jax.ShapeDtypeStruct((b, ATTN_HEADS, s, HEAD_W), BF16),
            t_shape(V_ROWS),
            jax.ShapeDtypeStruct((b, s, CONV_W), F32),
        ],
        compiler_params=pltpu.CompilerParams(
            dimension_semantics=("arbitrary", "arbitrary"),
            vmem_limit_bytes=V7X_VMEM_LIMIT_BYTES),
        name="in_proj",
    )(x, mod, g, wt, wc, cos_t, sin_t)


def _attn_kernel(qt_ref, k_ref, vt_ref, lq1_ref, lk1_ref, lq2_ref, lk2_ref, g_ref, o_ref,
                 qbd_scr, m_scr, acc_scr, s0, s1, p0, p1, a0, a1, *, tile, lambda_init):
    i = pl.program_id(2)
    groups = [slice(c0, c0 + ATTN_GROUP) for c0 in range(0, 2 * tile, ATTN_GROUP)]

    qt = qt_ref[0, 0, 0]
    first = lax.broadcasted_iota(jnp.int32, qt.shape, 0) < HEAD_DIM
    zero = jnp.zeros_like(qt)
    qbd_scr[:, 0:tile] = jnp.where(first, qt, zero)
    qbd_scr[:, tile:2 * tile] = jnp.where(first, zero, qt)

    m_scr[...] = jnp.full(m_scr.shape, MASK_VALUE, F32)
    acc_scr[...] = jnp.zeros(acc_scr.shape, F32)
    p1[...] = jnp.zeros(p1.shape, BF16)
    a1[...] = jnp.ones(a1.shape, F32)

    def scores(j, s_ref, cols):
        start = pl.multiple_of(j * tile, tile)
        kb = k_ref[0, 0, pl.ds(start, tile), :]
        s_ref[:, cols] = jnp.dot(kb, qbd_scr[:, cols], preferred_element_type=F32)

    def softmax(s_ref, p_ref, a_ref, cols, masked):
        s = s_ref[:, cols]
        if masked:
            key = lax.broadcasted_iota(jnp.int32, s.shape, 0)
            query = lax.broadcasted_iota(jnp.int32, s.shape, 1) + (cols.start % tile)
            s = jnp.where(key <= query, s, MASK_VALUE)
        m_prev = m_scr[:, cols]
        m_new = jnp.maximum(m_prev, jnp.max(s, axis=0, keepdims=True))
        p_ref[:, cols] = jnp.exp2(s - m_new).astype(BF16)
        a_ref[:, cols] = jnp.exp2(m_prev - m_new)
        m_scr[:, cols] = m_new

    def accumulate(j, p_ref, a_ref, cols):
        acc_scr[:, cols] = a_ref[:, cols] * acc_scr[:, cols] + jnp.dot(
            vt_ref[0, 0, j], p_ref[:, cols], preferred_element_type=F32)

    even = (s0, p0, a0)
    odd = (s1, p1, a1)

    def substep(j, cur, other):
        j_prev = jnp.maximum(j - 1, 0)
        for cols in groups:
            scores(j + 1, other[0], cols)
            softmax(cur[0], cur[1], cur[2], cols, False)
            accumulate(j_prev, other[1], other[2], cols)

    def last_block(cur, other):
        j_prev = jnp.maximum(i - 1, 0)
        for cols in groups:
            softmax(cur[0], cur[1], cur[2], cols, True)
            accumulate(j_prev, other[1], other[2], cols)
        for cols in groups:
            accumulate(i, cur[1], cur[2], cols)

    for cols in groups:
        scores(0, s0, cols)

    def pair(jj, carry):
        substep(2 * jj, even, odd)
        substep(2 * jj + 1, odd, even)
        return carry

    lax.fori_loop(0, lax.shift_right_logical(i, 1), pair, 0)

    @pl.when((i & 1) == 1)
    def _():
        substep(i - 1, even, odd)
        last_block(odd, even)

    @pl.when((i & 1) == 0)
    def _():
        last_block(even, odd)

    lam = (jnp.exp(jnp.sum(lq1_ref[...] * lk1_ref[...], axis=-1, keepdims=True))
           - jnp.exp(jnp.sum(lq2_ref[...] * lk2_ref[...], axis=-1, keepdims=True))
           + lambda_init)
    o1 = acc_scr[0:HEAD_W, 0:tile] / acc_scr[HEAD_W:HEAD_W + 1, 0:tile]
    o2 = acc_scr[0:HEAD_W, tile:2 * tile] / acc_scr[HEAD_W:HEAD_W + 1, tile:2 * tile]
    o = o1 - lam * o2
    o = o * lax.rsqrt(jnp.mean(o * o, axis=0, keepdims=True) + 1e-5)
    o = (o * g_ref[...]) * (1.0 - lambda_init)
    o_ref[0] = o.T.astype(BF16)


def _attn_call(qt, k, vt, lq1, lk1, lq2, lk2, subln_g_col, lambda_init):
    b, nh, s, _ = k.shape
    tile = qt.shape[-1]
    vec = lambda n: pl.BlockSpec((1, n), lambda bi, hi, i: (0, 0))
    return pl.pallas_call(
        functools.partial(_attn_kernel, tile=tile, lambda_init=lambda_init),
        grid=(b, nh, s // tile),
        in_specs=[
            pl.BlockSpec((1, 1, 1, HEAD_W, tile), lambda bi, hi, i: (bi, hi, i, 0, 0)),
            pl.BlockSpec((1, 1, s, HEAD_W), lambda bi, hi, i: (bi, hi, 0, 0)),
            pl.BlockSpec((1, 1, s // tile, V_ROWS, tile), lambda bi, hi, i: (bi, hi, 0, 0, 0)),
            vec(HEAD_DIM), vec(HEAD_DIM), vec(HEAD_DIM), vec(HEAD_DIM),
            pl.BlockSpec((HEAD_W, 1), lambda bi, hi, i: (0, 0)),
        ],
        out_specs=pl.BlockSpec((1, tile, HEAD_W), lambda bi, hi, i: (bi, i, hi)),
        out_shape=jax.ShapeDtypeStruct((b, s, nh * HEAD_W), BF16),
        scratch_shapes=[
            pltpu.VMEM((HEAD_W, 2 * tile), BF16),
            pltpu.VMEM((1, 2 * tile), F32),
            pltpu.VMEM((V_ROWS, 2 * tile), F32),
            pltpu.VMEM((tile, 2 * tile), F32), pltpu.VMEM((tile, 2 * tile), F32),
            pltpu.VMEM((tile, 2 * tile), BF16), pltpu.VMEM((tile, 2 * tile), BF16),
            pltpu.VMEM((1, 2 * tile), F32), pltpu.VMEM((1, 2 * tile), F32),
        ],
        compiler_params=pltpu.CompilerParams(
            dimension_semantics=("arbitrary", "arbitrary", "arbitrary"),
            vmem_limit_bytes=V7X_VMEM_LIMIT_BYTES),
        name="diff_attn",
    )(qt, k, vt, lq1, lk1, lq2, lk2, subln_g_col)


def _mixer_kernel(x_ref, mod_ref, g_ref, wg_ref, o_ref, u_ref, uh_ref, cw_ref, cb_ref, lg_ref, lb_ref,
                  wao_ref, wco_ref, wout_ref, x1_ref, ubuf, *, tm):
    i = pl.program_id(1)
    d = x_ref.shape[-1]
    x = x_ref[0]
    h = _modulated_rms_norm(x, g_ref[...], mod_ref[0, 1:2, :], mod_ref[0, 0:1, :]).astype(BF16)
    halo = uh_ref[0]
    ubuf[0:CONV_HALO, :] = jnp.where(i > 0, halo, jnp.zeros_like(halo))
    ubuf[CONV_HALO:CONV_HALO + tm, :] = u_ref[0]
    ubuf[CONV_HALO + tm:CONV_HALO + tm + SUBLANES, :] = jnp.zeros((SUBLANES, CONV_W), F32)
    base = CONV_HALO - (CONV_K - 1)

    def conv_lanes(lanes):
        out = cb_ref[:, lanes]
        for r in range(SUBLANES):
            part = None
            for a in range((base + CONV_K - 1 - r) // SUBLANES + 1):
                j = SUBLANES * a + r - base
                if j < 0:
                    continue
                term = cw_ref[j:j + 1, lanes] * ubuf[SUBLANES * a:SUBLANES * a + tm + SUBLANES, lanes]
                part = term if part is None else part + term
            out = out + part[r:r + tm, :]
        return out

    n_chunks = CONV_W // LANES
    gate_cols = wg_ref.shape[1] // n_chunks
    gate_parts, conv_parts = [], []
    for c in range(n_chunks):
        z = jnp.dot(h, wg_ref[:, c * gate_cols:(c + 1) * gate_cols], preferred_element_type=F32)
        gate_parts.append(_sigmoid(z))
        conv_parts.append(conv_lanes(slice(c * LANES, (c + 1) * LANES)))
    gates = jnp.concatenate(gate_parts, axis=1)
    conv = jnp.concatenate(conv_parts, axis=1)

    mu = jnp.mean(conv, axis=-1, keepdims=True)
    cen = conv - mu
    var = jnp.mean(cen * cen, axis=-1, keepdims=True)
    y = (cen * lax.rsqrt(var + 1e-5)) * lg_ref[...] + lb_ref[...]
    act = (y * _sigmoid(y)).astype(BF16)

    y_conv = jnp.dot(act, wco_ref[...], preferred_element_type=F32)
    y_attn = jnp.dot(o_ref[0], wao_ref[...], preferred_element_type=F32)
    mix = (gates[:, 0:d] * y_attn + gates[:, d:2 * d] * y_conv).astype(BF16)
    mixed = jnp.dot(mix, wout_ref[...], preferred_element_type=F32)
    x1_ref[0] = x + (1.0 + mod_ref[0, 2:3, :]) * mixed


def _mixer_call(x, mod, g, wg, o, u, cw, cb, lg, lb, wao, wco, wout):
    b, s, d = x.shape
    tm = min(ROW_TILE, s)
    halo_blocks = tm // CONV_HALO
    row = lambda bi, i: (bi, i, 0)
    return pl.pallas_call(
        functools.partial(_mixer_kernel, tm=tm),
        grid=(b, s // tm),
        in_specs=[
            pl.BlockSpec((1, tm, d), row),
            pl.BlockSpec((1, N_MOD, d), lambda bi, i: (bi, 0, 0)),
            _resident((1, d)),
            _resident(wg.shape),
            pl.BlockSpec((1, tm, ATTN_W), row),
            pl.BlockSpec((1, tm, CONV_W), row),
            pl.BlockSpec((1, CONV_HALO, CONV_W), lambda bi, i: (bi, jnp.maximum(i * halo_blocks - 1, 0), 0)),
            _resident(cw.shape), _resident(cb.shape), _resident(lg.shape), _resident(lb.shape),
            _resident(wao.shape), _resident(wco.shape), _resident(wout.shape),
        ],
        out_specs=pl.BlockSpec((1, tm, d), row),
        out_shape=jax.ShapeDtypeStruct((b, s, d), F32),
        scratch_shapes=[pltpu.VMEM((CONV_HALO + tm + SUBLANES, CONV_W), F32)],
        compiler_params=pltpu.CompilerParams(
            dimension_semantics=("arbitrary", "arbitrary"),
            vmem_limit_bytes=V7X_VMEM_LIMIT_BYTES),
        name="mixer",
    )(x, mod, g, wg, o, u, u, cw, cb, lg, lb, wao, wco, wout)


def _ffn_kernel(x_ref, mod_ref, g_ref, wi_ref, wo_ref, fg_ref, out_ref, *, d_ff, final):
    x = x_ref[0]
    h = _modulated_rms_norm(x, g_ref[...], mod_ref[0, 4:5, :], mod_ref[0, 3:4, :]).astype(BF16)
    acc = jnp.zeros(x.shape, F32)
    for c0 in range(0, d_ff, FFN_CHUNK):
        cw = min(FFN_CHUNK, d_ff - c0)
        f_gate = jnp.dot(h, wi_ref[:, c0:c0 + cw], preferred_element_type=F32)
        f_up = jnp.dot(h, wi_ref[:, d_ff + c0:d_ff + c0 + cw], preferred_element_type=F32)
        a = ((f_gate * _sigmoid(f_gate)) * f_up).astype(BF16)
        acc = acc + jnp.dot(a, wo_ref[c0:c0 + cw, :], preferred_element_type=F32)
    x2 = x + (1.0 + mod_ref[0, 5:6, :]) * acc
    if final:
        x2 = (x2 * lax.rsqrt(jnp.mean(x2 * x2, axis=-1, keepdims=True) + 1e-6)) * fg_ref[...]
    out_ref[0] = x2


def _ffn_call(x, mod, g, wi, wo, final_g, final):
    b, s, d = x.shape
    d_ff = wo.shape[0]
    tm = min(ROW_TILE, s)
    row = lambda bi, i: (bi, i, 0)
    return pl.pallas_call(
        functools.partial(_ffn_kernel, d_ff=d_ff, final=final),
        grid=(b, s // tm),
        in_specs=[
            pl.BlockSpec((1, tm, d), row),
            pl.BlockSpec((1, N_MOD, d), lambda bi, i: (bi, 0, 0)),
            _resident((1, d)),
            _resident(wi.shape),
            _resident(wo.shape),
            _resident((1, d)),
        ],
        out_specs=pl.BlockSpec((1, tm, d), row),
        out_shape=jax.ShapeDtypeStruct((b, s, d), F32),
        compiler_params=pltpu.CompilerParams(
            dimension_semantics=("arbitrary", "arbitrary"),
            vmem_limit_bytes=V7X_VMEM_LIMIT_BYTES),
        name="ffn_final" if final else "ffn",
    )(x, mod, g, wi, wo, final_g)


def _rope_tables(positions):
    inv_freq = 1.0 / (ROPE_THETA ** (jnp.arange(0, ROT_DIM, 2, dtype=F32) / ROT_DIM))
    ang = positions.astype(F32)[..., None] * inv_freq
    return jnp.swapaxes(jnp.cos(ang), 1, 2), jnp.swapaxes(jnp.sin(ang), 1, 2)


def kernel(x, c, positions, ada_w, ada_b, norm1_g, w_in, lambda_q1, lambda_k1, lambda_q2, lambda_k2,
           subln_g, w_attn_o, dw_conv_w, dw_conv_b, conv_ln_g, conv_ln_b, w_conv_o, w_out, norm2_g,
           w_ffn_in, w_ffn_out, final_g):
    b, s, d = x.shape
    depth = ada_w.shape[0]
    assert s % min(ROW_TILE, s) == 0 and s % min(ATTN_TILE, s) == 0
    assert (2 * min(ATTN_TILE, s)) % ATTN_GROUP == 0
    assert ada_w.shape[2] == N_MOD * d and ada_w.shape[2] % ADA_COLS == 0
    assert w_in.shape[2] == N_QKVU + 2 * d

    rows = -(-b // 8) * 8
    c_pad = jnp.zeros((rows, d), F32).at[:b].set(c)
    mod_all = _ada_call(c_pad, ada_w, ada_b)[:, :b].reshape(depth, b, N_MOD, d)

    cos_t, sin_t = _rope_tables(positions)
    final_row = final_g.reshape(1, d)

    for l in range(depth):
        lambda_init = 0.8 - 0.6 * math.exp(-0.3 * l)
        mod = mod_all[l]
        g1 = norm1_g[l].reshape(1, d)
        w_l = w_in[l]
        qt, k, vt, u = _inproj_call(x, mod, g1, w_l[:, :3 * ATTN_W].T.astype(BF16),
                                    w_l[:, 3 * ATTN_W:N_QKVU].astype(BF16), cos_t, sin_t)
        o = _attn_call(qt, k, vt,
                       lambda_q1[l].reshape(1, HEAD_DIM), lambda_k1[l].reshape(1, HEAD_DIM),
                       lambda_q2[l].reshape(1, HEAD_DIM), lambda_k2[l].reshape(1, HEAD_DIM),
                       subln_g[l].reshape(HEAD_W, 1), lambda_init)
        x = _mixer_call(x, mod, g1, w_l[:, N_QKVU:].astype(BF16), o, u,
                        dw_conv_w[l].reshape(CONV_K, CONV_W), dw_conv_b[l].reshape(1, CONV_W),
                        conv_ln_g[l].reshape(1, CONV_W), conv_ln_b[l].reshape(1, CONV_W),
                        w_attn_o[l].astype(BF16), w_conv_o[l].astype(BF16), w_out[l].astype(BF16))
        x = _ffn_call(x, mod, norm2_g[l].reshape(1, d), w_ffn_in[l].astype(BF16),
                      w_ffn_out[l].astype(BF16), final_row, final=(l == depth - 1))
    return x
```

```python
import collections
import functools
import math

import jax
import jax.numpy as jnp
from jax import lax
from jax.experimental import pallas as pl
from jax.experimental.pallas import tpu as pltpu

F32 = jnp.float32
BF16 = jnp.bfloat16

ATTN_HEADS = 4
HEAD_DIM = 64
HEAD_W = 2 * HEAD_DIM
ATTN_W = ATTN_HEADS * HEAD_W
ROT_DIM = HEAD_DIM // 4
ROPE_THETA = 500000.0
CONV_W = 512
CONV_K = 31
N_MOD = 6
N_QKVU = 3 * ATTN_W + 2 * CONV_W
CONV_HALO = 32
MASK_VALUE = -1e30
Q_SCALE = math.log2(math.e) / math.sqrt(HEAD_DIM)
V_ROWS = HEAD_W + 16

V7X_VMEM_LIMIT_BYTES = 56 * 1024 * 1024
V7X_MXU_COLS = 256
SUBLANES = 8
LANES = 128

ROW_TILE = 512
ATTN_TILE = 512
ATTN_GROUP = V7X_MXU_COLS
ADA_COLS = 1024
FFN_CHUNK = 1024


def _sigmoid(x):
    return jax.nn.sigmoid(x)


def _modulated_rms_norm(x, g, scale, shift, eps=1e-6):
    y = x * lax.rsqrt(jnp.mean(x * x, axis=-1, keepdims=True) + eps)
    return (y * g) * (1.0 + scale) + shift


def _resident(shape):
    zeros = (0,) * len(shape)
    return pl.BlockSpec(shape, lambda *_: zeros, pipeline_mode=pl.Buffered(1))


def _ada_kernel(c_ref, w_ref, b_ref, o_ref):
    c = c_ref[...]
    c_act = (c * _sigmoid(c)).astype(BF16)
    w = w_ref[0].astype(BF16)
    o_ref[0] = jnp.dot(c_act, w, preferred_element_type=F32) + b_ref[0]


def _ada_call(c_pad, ada_w, ada_b):
    depth, d, n = ada_w.shape
    rows = c_pad.shape[0]
    return pl.pallas_call(
        _ada_kernel,
        grid=(depth, n // ADA_COLS),
        in_specs=[
            pl.BlockSpec((rows, d), lambda l, j: (0, 0)),
            pl.BlockSpec((1, d, ADA_COLS), lambda l, j: (l, 0, j)),
            pl.BlockSpec((1, 1, ADA_COLS), lambda l, j: (l, 0, j)),
        ],
        out_specs=pl.BlockSpec((1, rows, ADA_COLS), lambda l, j: (l, 0, j)),
        out_shape=jax.ShapeDtypeStruct((depth, rows, n), F32),
        compiler_params=pltpu.CompilerParams(
            dimension_semantics=("arbitrary", "arbitrary"),
            vmem_limit_bytes=V7X_VMEM_LIMIT_BYTES),
        name="ada_proj",
    )(c_pad, ada_w, ada_b.reshape(depth, 1, n))


_NT_DIMS = (((1,), (1,)), ((), ()))

_StageBuffers = collections.namedtuple("_StageBuffers", ["s", "smax"])


def _rope_rows(z, cos, sin):
    half = ROT_DIM // 2
    parts = []
    for comp in range(2):
        r0 = comp * HEAD_DIM
        x1 = z[r0:r0 + half]
        x2 = z[r0 + half:r0 + ROT_DIM]
        parts += [x1 * cos - x2 * sin, x2 * cos + x1 * sin, z[r0 + ROT_DIM:r0 + HEAD_DIM]]
    return jnp.concatenate(parts, axis=0)


def _inproj_kernel(x_ref, mod_ref, g_ref, wt_ref, wc_ref, cos_ref, sin_ref,
                   qt_ref, k_ref, vt_ref, u_ref):
    x = x_ref[0]
    h = _modulated_rms_norm(x, g_ref[...], mod_ref[0, 1:2, :], mod_ref[0, 0:1, :]).astype(BF16)
    cos = cos_ref[0]
    sin = sin_ref[0]

    zq = lax.dot_general(wt_ref[0:ATTN_W, :], h, _NT_DIMS, preferred_element_type=F32)
    for hd in range(ATTN_HEADS):
        t = _rope_rows(zq[hd * HEAD_W:(hd + 1) * HEAD_W], cos, sin)
        qt_ref[0, hd, 0] = (t * Q_SCALE).astype(BF16)

    zk = lax.dot_general(wt_ref[ATTN_W:2 * ATTN_W, :], h, _NT_DIMS, preferred_element_type=F32)
    for hd in range(ATTN_HEADS):
        k_ref[0, hd] = _rope_rows(zk[hd * HEAD_W:(hd + 1) * HEAD_W], cos, sin).T.astype(BF16)

    zv = lax.dot_general(wt_ref[2 * ATTN_W:3 * ATTN_W, :], h, _NT_DIMS, preferred_element_type=F32)
    for hd in range(ATTN_HEADS):
        vt_ref[0, hd, 0, 0:HEAD_W, :] = zv[hd * HEAD_W:(hd + 1) * HEAD_W].astype(BF16)
        vt_ref[0, hd, 0, HEAD_W:V_ROWS, :] = jnp.ones((V_ROWS - HEAD_W, zv.shape[1]), BF16)

    ga = jnp.dot(h, wc_ref[:, 0:CONV_W], preferred_element_type=F32)
    gb = jnp.dot(h, wc_ref[:, CONV_W:2 * CONV_W], preferred_element_type=F32)
    u_ref[0] = ga * _sigmoid(gb)


def _inproj_call(x, mod, g, wt, wc, cos_t, sin_t):
    b, s, d = x.shape
    tm = min(ATTN_TILE, s)
    row = lambda bi, i: (bi, i, 0)
    t_spec = lambda rows: pl.BlockSpec((1, ATTN_HEADS, 1, rows, tm), lambda bi, i: (bi, 0, i, 0, 0))
    t_shape = lambda rows: jax.ShapeDtypeStruct((b, ATTN_HEADS, s // tm, rows, tm), BF16)
    rot_spec = pl.BlockSpec((1, ROT_DIM // 2, tm), lambda bi, i: (bi, 0, i))
    return pl.pallas_call(
        _inproj_kernel,
        grid=(b, s // tm),
        in_specs=[
            pl.BlockSpec((1, tm, d), row),
            pl.BlockSpec((1, N_MOD, d), lambda bi, i: (bi, 0, 0)),
            _resident((1, d)),
            _resident(wt.shape),
            _resident(wc.shape),
            rot_spec, rot_spec,
        ],
        out_specs=[
            t_spec(HEAD_W),
            pl.BlockSpec((1, ATTN_HEADS, tm, HEAD_W), lambda bi, i: (bi, 0, i, 0)),
            t_spec(V_ROWS),
            pl.BlockSpec((1, tm, CONV_W), row),
        ],
        out_shape=[
            t_shape(HEAD_W),
            jax.ShapeDtypeStruct((b, ATTN_HEADS, s, HEAD_W), BF16),
            t_shape(V_ROWS),
            jax.ShapeDtypeStruct((b, s, CONV_W), F32),
        ],
        compiler_params=pltpu.CompilerParams(
            dimension_semantics=("arbitrary", "arbitrary"),
            vmem_limit_bytes=V7X_VMEM_LIMIT_BYTES),
        name="in_proj",
    )(x, mod, g, wt, wc, cos_t, sin_t)


def _attn_kernel(qt_ref, k_ref, vt_ref, lq1_ref, lk1_ref, lq2_ref, lk2_ref, g_ref, o_ref,
                 qbd_scr, m_scr, acc_scr, s0, s1, x0, x1, *, tile, lambda_init):
    i = pl.program_id(2)
    groups = [slice(c0, c0 + ATTN_GROUP) for c0 in range(0, 2 * tile, ATTN_GROUP)]

    qt = qt_ref[0, 0, 0]
    first = lax.broadcasted_iota(jnp.int32, qt.shape, 0) < HEAD_DIM
    zero = jnp.zeros_like(qt)
    qbd_scr[:, 0:tile] = jnp.where(first, qt, zero)
    qbd_scr[:, tile:2 * tile] = jnp.where(first, zero, qt)

    m_scr[...] = jnp.full(m_scr.shape, MASK_VALUE, F32)
    acc_scr[...] = jnp.zeros(acc_scr.shape, F32)

    def scores(j, buf, cols):
        start = pl.multiple_of(j * tile, tile)
        kb = k_ref[0, 0, pl.ds(start, tile), :]
        s = jnp.dot(kb, qbd_scr[:, cols], preferred_element_type=F32)
        buf.s[:, cols] = s
        buf.smax[:, cols] = jnp.max(s, axis=0, keepdims=True)

    def softmax(j, buf, cols, masked):
        if masked:
            first_query = cols.start % tile
            keys = min(tile, first_query + ATTN_GROUP)
            s = buf.s[0:keys, cols]
            key = lax.broadcasted_iota(jnp.int32, s.shape, 0)
            query = lax.broadcasted_iota(jnp.int32, s.shape, 1) + first_query
            s = jnp.where(key <= query, s, MASK_VALUE)
            block_max = jnp.max(s, axis=0, keepdims=True)
        else:
            keys = tile
            s = buf.s[:, cols]
            block_max = buf.smax[:, cols]
        m_prev = m_scr[:, cols]
        m_new = jnp.maximum(m_prev, block_max)
        p = jnp.exp2(s - m_new).astype(BF16)
        alpha = jnp.exp2(m_prev - m_new)
        m_scr[:, cols] = m_new
        acc_scr[:, cols] = alpha * acc_scr[:, cols] + jnp.dot(
            vt_ref[0, 0, j, :, 0:keys], p, preferred_element_type=F32)

    even = _StageBuffers(s0, x0)
    odd = _StageBuffers(s1, x1)

    def substep(j, cur, other):
        for cols in groups:
            scores(j + 1, other, cols)
            softmax(j, cur, cols, False)

    def last_block(cur, other):
        for cols in groups:
            softmax(i, cur, cols, True)

    for cols in groups:
        scores(0, even, cols)

    def pair(jj, carry):
        substep(2 * jj, even, odd)
        substep(2 * jj + 1, odd, even)
        return carry

    lax.fori_loop(0, lax.shift_right_logical(i, 1), pair, 0)

    @pl.when((i & 1) == 1)
    def _():
        substep(i - 1, even, odd)
        last_block(odd, even)

    @pl.when((i & 1) == 0)
    def _():
        last_block(even, odd)

    lam = (jnp.exp(jnp.sum(lq1_ref[...] * lk1_ref[...], axis=-1, keepdims=True))
           - jnp.exp(jnp.sum(lq2_ref[...] * lk2_ref[...], axis=-1, keepdims=True))
           + lambda_init)
    o1 = acc_scr[0:HEAD_W, 0:tile] / acc_scr[HEAD_W:HEAD_W + 1, 0:tile]
    o2 = acc_scr[0:HEAD_W, tile:2 * tile] / acc_scr[HEAD_W:HEAD_W + 1, tile:2 * tile]
    o = o1 - lam * o2
    o = o * lax.rsqrt(jnp.mean(o * o, axis=0, keepdims=True) + 1e-5)
    o = (o * g_ref[...]) * (1.0 - lambda_init)
    o_ref[0] = o.T.astype(BF16)


def _attn_call(qt, k, vt, lq1, lk1, lq2, lk2, subln_g_col, lambda_init):
    b, nh, s, _ = k.shape
    tile = qt.shape[-1]
    vec = lambda n: pl.BlockSpec((1, n), lambda bi, hi, i: (0, 0))
    return pl.pallas_call(
        functools.partial(_attn_kernel, tile=tile, lambda_init=lambda_init),
        grid=(b, nh, s // tile),
        in_specs=[
            pl.BlockSpec((1, 1, 1, HEAD_W, tile), lambda bi, hi, i: (bi, hi, i, 0, 0)),
            pl.BlockSpec((1, 1, s, HEAD_W), lambda bi, hi, i: (bi, hi, 0, 0)),
            pl.BlockSpec((1, 1, s // tile, V_ROWS, tile), lambda bi, hi, i: (bi, hi, 0, 0, 0)),
            vec(HEAD_DIM), vec(HEAD_DIM), vec(HEAD_DIM), vec(HEAD_DIM),
            pl.BlockSpec((HEAD_W, 1), lambda bi, hi, i: (0, 0)),
        ],
        out_specs=pl.BlockSpec((1, tile, HEAD_W), lambda bi, hi, i: (bi, i, hi)),
        out_shape=jax.ShapeDtypeStruct((b, s, nh * HEAD_W), BF16),
        scratch_shapes=[
            pltpu.VMEM((HEAD_W, 2 * tile), BF16),
            pltpu.VMEM((1, 2 * tile), F32),
            pltpu.VMEM((V_ROWS, 2 * tile), F32),
            pltpu.VMEM((tile, 2 * tile), F32), pltpu.VMEM((tile, 2 * tile), F32),
            pltpu.VMEM((1, 2 * tile), F32), pltpu.VMEM((1, 2 * tile), F32),
        ],
        compiler_params=pltpu.CompilerParams(
            dimension_semantics=("arbitrary", "arbitrary", "arbitrary"),
            vmem_limit_bytes=V7X_VMEM_LIMIT_BYTES),
        name="diff_attn",
    )(qt, k, vt, lq1, lk1, lq2, lk2, subln_g_col)


def _mixer_kernel(x_ref, mod_ref, g_ref, wg_ref, o_ref, u_ref, uh_ref, cw_ref, cb_ref, lg_ref, lb_ref,
                  wao_ref, wco_ref, wout_ref, x1_ref, ubuf, *, tm):
    i = pl.program_id(1)
    d = x_ref.shape[-1]
    x = x_ref[0]
    h = _modulated_rms_norm(x, g_ref[...], mod_ref[0, 1:2, :], mod_ref[0, 0:1, :]).astype(BF16)
    halo = uh_ref[0]
    ubuf[0:CONV_HALO, :] = jnp.where(i > 0, halo, jnp.zeros_like(halo))
    ubuf[CONV_HALO:CONV_HALO + tm, :] = u_ref[0]
    ubuf[CONV_HALO + tm:CONV_HALO + tm + SUBLANES, :] = jnp.zeros((SUBLANES, CONV_W), F32)
    base = CONV_HALO - (CONV_K - 1)

    def conv_lanes(lanes):
        out = cb_ref[:, lanes]
        for r in range(SUBLANES):
            part = None
            for a in range((base + CONV_K - 1 - r) // SUBLANES + 1):
                j = SUBLANES * a + r - base
                if j < 0:
                    continue
                term = cw_ref[j:j + 1, lanes] * ubuf[SUBLANES * a:SUBLANES * a + tm + SUBLANES, lanes]
                part = term if part is None else part + term
            out = out + part[r:r + tm, :]
        return out

    n_chunks = CONV_W // LANES
    gate_cols = wg_ref.shape[1] // n_chunks
    gate_parts, conv_parts = [], []
    for c in range(n_chunks):
        z = jnp.dot(h, wg_ref[:, c * gate_cols:(c + 1) * gate_cols], preferred_element_type=F32)
        gate_parts.append(_sigmoid(z))
        conv_parts.append(conv_lanes(slice(c * LANES, (c + 1) * LANES)))
    gates = jnp.concatenate(gate_parts, axis=1)
    conv = jnp.concatenate(conv_parts, axis=1)

    mu = jnp.mean(conv, axis=-1, keepdims=True)
    cen = conv - mu
    var = jnp.mean(cen * cen, axis=-1, keepdims=True)
    y = (cen * lax.rsqrt(var + 1e-5)) * lg_ref[...] + lb_ref[...]
    act = (y * _sigmoid(y)).astype(BF16)

    y_conv = jnp.dot(act, wco_ref[...], preferred_element_type=F32)
    y_attn = jnp.dot(o_ref[0], wao_ref[...], preferred_element_type=F32)
    mix = (gates[:, 0:d] * y_attn + gates[:, d:2 * d] * y_conv).astype(BF16)
    mixed = jnp.dot(mix, wout_ref[...], preferred_element_type=F32)
    x1_ref[0] = x + (1.0 + mod_ref[0, 2:3, :]) * mixed


def _mixer_call(x, mod, g, wg, o, u, cw, cb, lg, lb, wao, wco, wout):
    b, s, d = x.shape
    tm = min(ROW_TILE, s)
    halo_blocks = tm // CONV_HALO
    row = lambda bi, i: (bi, i, 0)
    return pl.pallas_call(
        functools.partial(_mixer_kernel, tm=tm),
        grid=(b, s // tm),
        in_specs=[
            pl.BlockSpec((1, tm, d), row),
            pl.BlockSpec((1, N_MOD, d), lambda bi, i: (bi, 0, 0)),
            _resident((1, d)),
            _resident(wg.shape),
            pl.BlockSpec((1, tm, ATTN_W), row),
            pl.BlockSpec((1, tm, CONV_W), row),
            pl.BlockSpec((1, CONV_HALO, CONV_W), lambda bi, i: (bi, jnp.maximum(i * halo_blocks - 1, 0), 0)),
            _resident(cw.shape), _resident(cb.shape), _resident(lg.shape), _resident(lb.shape),
            _resident(wao.shape), _resident(wco.shape), _resident(wout.shape),
        ],
        out_specs=pl.BlockSpec((1, tm, d), row),
        out_shape=jax.ShapeDtypeStruct((b, s, d), F32),
        scratch_shapes=[pltpu.VMEM((CONV_HALO + tm + SUBLANES, CONV_W), F32)],
        compiler_params=pltpu.CompilerParams(
            dimension_semantics=("arbitrary", "arbitrary"),
            vmem_limit_bytes=V7X_VMEM_LIMIT_BYTES),
        name="mixer",
    )(x, mod, g, wg, o, u, u, cw, cb, lg, lb, wao, wco, wout)


def _ffn_kernel(x_ref, mod_ref, g_ref, wi_ref, wo_ref, fg_ref, out_ref, *, d_ff, final):
    x = x_ref[0]
    h = _modulated_rms_norm(x, g_ref[...], mod_ref[0, 4:5, :], mod_ref[0, 3:4, :]).astype(BF16)
    acc = jnp.zeros(x.shape, F32)
    for c0 in range(0, d_ff, FFN_CHUNK):
        cw = min(FFN_CHUNK, d_ff - c0)
        f_gate = jnp.dot(h, wi_ref[:, c0:c0 + cw], preferred_element_type=F32)
        f_up = jnp.dot(h, wi_ref[:, d_ff + c0:d_ff + c0 + cw], preferred_element_type=F32)
        a = ((f_gate * _sigmoid(f_gate)) * f_up).astype(BF16)
        acc = acc + jnp.dot(a, wo_ref[c0:c0 + cw, :], preferred_element_type=F32)
    x2 = x + (1.0 + mod_ref[0, 5:6, :]) * acc
    if final:
        x2 = (x2 * lax.rsqrt(jnp.mean(x2 * x2, axis=-1, keepdims=True) + 1e-6)) * fg_ref[...]
    out_ref[0] = x2


def _ffn_call(x, mod, g, wi, wo, final_g, final):
    b, s, d = x.shape
    d_ff = wo.shape[0]
    tm = min(ROW_TILE, s)
    row = lambda bi, i: (bi, i, 0)
    return pl.pallas_call(
        functools.partial(_ffn_kernel, d_ff=d_ff, final=final),
        grid=(b, s // tm),
        in_specs=[
            pl.BlockSpec((1, tm, d), row),
            pl.BlockSpec((1, N_MOD, d), lambda bi, i: (bi, 0, 0)),
            _resident((1, d)),
            _resident(wi.shape),
            _resident(wo.shape),
            _resident((1, d)),
        ],
        out_specs=pl.BlockSpec((1, tm, d), row),
        out_shape=jax.ShapeDtypeStruct((b, s, d), F32),
        compiler_params=pltpu.CompilerParams(
            dimension_semantics=("arbitrary", "arbitrary"),
            vmem_limit_bytes=V7X_VMEM_LIMIT_BYTES),
        name="ffn_final" if final else "ffn",
    )(x, mod, g, wi, wo, final_g)


def _rope_tables(positions):
    inv_freq = 1.0 / (ROPE_THETA ** (jnp.arange(0, ROT_DIM, 2, dtype=F32) / ROT_DIM))
    ang = positions.astype(F32)[..., None] * inv_freq
    return jnp.swapaxes(jnp.cos(ang), 1, 2), jnp.swapaxes(jnp.sin(ang), 1, 2)


def kernel(x, c, positions, ada_w, ada_b, norm1_g, w_in, lambda_q1, lambda_k1, lambda_q2, lambda_k2,
           subln_g, w_attn_o, dw_conv_w, dw_conv_b, conv_ln_g, conv_ln_b, w_conv_o, w_out, norm2_g,
           w_ffn_in, w_ffn_out, final_g):
    b, s, d = x.shape
    depth = ada_w.shape[0]
    assert s % min(ROW_TILE, s) == 0 and s % min(ATTN_TILE, s) == 0
    assert (2 * min(ATTN_TILE, s)) % ATTN_GROUP == 0
    assert ada_w.shape[2] == N_MOD * d and ada_w.shape[2] % ADA_COLS == 0
    assert w_in.shape[2] == N_QKVU + 2 * d

    rows = -(-b // 8) * 8
    c_pad = jnp.zeros((rows, d), F32).at[:b].set(c)
    mod_all = _ada_call(c_pad, ada_w, ada_b)[:, :b].reshape(depth, b, N_MOD, d)

    cos_t, sin_t = _rope_tables(positions)
    final_row = final_g.reshape(1, d)

    for l in range(depth):
        lambda_init = 0.8 - 0.6 * math.exp(-0.3 * l)
        mod = mod_all[l]
        g1 = norm1_g[l].reshape(1, d)
        w_l = w_in[l]
        qt, k, vt, u = _inproj_call(x, mod, g1, w_l[:, :3 * ATTN_W].T.astype(BF16),
                                    w_l[:, 3 * ATTN_W:N_QKVU].astype(BF16), cos_t, sin_t)
        o = _attn_call(qt, k, vt,
                       lambda_q1[l].reshape(1, HEAD_DIM), lambda_k1[l].reshape(1, HEAD_DIM),
                       lambda_q2[l].reshape(1, HEAD_DIM), lambda_k2[l].reshape(1, HEAD_DIM),
                       subln_g[l].reshape(HEAD_W, 1), lambda_init)
        x = _mixer_call(x, mod, g1, w_l[:, N_QKVU:].astype(BF16), o, u,
                        dw_conv_w[l].reshape(CONV_K, CONV_W), dw_conv_b[l].reshape(1, CONV_W),
                        conv_ln_g[l].reshape(1, CONV_W), conv_ln_b[l].reshape(1, CONV_W),
                        w_attn_o[l].astype(BF16), w_conv_o[l].astype(BF16), w_out[l].astype(BF16))
        x = _ffn_call(x, mod, norm2_g[l].reshape(1, d), w_ffn_in[l].astype(BF16),
                      w_ffn_out[l].astype(BF16), final_row, final=(l == depth - 1))
    return x
```

```python
import collections
import functools
import math

import jax
import jax.numpy as jnp
from jax import lax
from jax.experimental import pallas as pl
from jax.experimental.pallas import tpu as pltpu

F32 = jnp.float32
BF16 = jnp.bfloat16

ATTN_HEADS = 4
HEAD_DIM = 64
HEAD_W = 2 * HEAD_DIM
ATTN_W = ATTN_HEADS * HEAD_W
ROT_DIM = HEAD_DIM // 4
ROPE_THETA = 500000.0
CONV_W = 512
CONV_K = 31
N_MOD = 6
N_QKVU = 3 * ATTN_W + 2 * CONV_W
CONV_HALO = 32
MASK_VALUE = -1e30
Q_SCALE = math.log2(math.e) / math.sqrt(HEAD_DIM)
V_ROWS = HEAD_W + 16

V7X_VMEM_LIMIT_BYTES = 56 * 1024 * 1024
V7X_MXU_COLS = 256
SUBLANES = 8
LANES = 128

ROW_TILE = 512
ATTN_TILE = 512
ATTN_GROUP = V7X_MXU_COLS
ADA_COLS = 1024
FFN_CHUNK = 1024


def _sigmoid(x):
    return jax.nn.sigmoid(x)


def _modulated_rms_norm(x, g, scale, shift, eps=1e-6):
    y = x * lax.rsqrt(jnp.mean(x * x, axis=-1, keepdims=True) + eps)
    return (y * g) * (1.0 + scale) + shift


def _resident(shape):
    zeros = (0,) * len(shape)
    return pl.BlockSpec(shape, lambda *_: zeros, pipeline_mode=pl.Buffered(1))


def _ada_kernel(c_ref, w_ref, b_ref, o_ref):
    c = c_ref[...]
    c_act = (c * _sigmoid(c)).astype(BF16)
    w = w_ref[0].astype(BF16)
    o_ref[0] = jnp.dot(c_act, w, preferred_element_type=F32) + b_ref[0]


def _ada_call(c_pad, ada_w, ada_b):
    depth, d, n = ada_w.shape
    rows = c_pad.shape[0]
    return pl.pallas_call(
        _ada_kernel,
        grid=(depth, n // ADA_COLS),
        in_specs=[
            pl.BlockSpec((rows, d), lambda l, j: (0, 0)),
            pl.BlockSpec((1, d, ADA_COLS), lambda l, j: (l, 0, j)),
            pl.BlockSpec((1, 1, ADA_COLS), lambda l, j: (l, 0, j)),
        ],
        out_specs=pl.BlockSpec((1, rows, ADA_COLS), lambda l, j: (l, 0, j)),
        out_shape=jax.ShapeDtypeStruct((depth, rows, n), F32),
        compiler_params=pltpu.CompilerParams(
            dimension_semantics=("arbitrary", "arbitrary"),
            vmem_limit_bytes=V7X_VMEM_LIMIT_BYTES),
        name="ada_proj",
    )(c_pad, ada_w, ada_b.reshape(depth, 1, n))


_NT_DIMS = (((1,), (1,)), ((), ()))

_StageBuffers = collections.namedtuple("_StageBuffers", ["s", "smax"])


def _rope_rows(z, cos, sin):
    half = ROT_DIM // 2
    parts = []
    for comp in range(2):
        r0 = comp * HEAD_DIM
        x1 = z[r0:r0 + half]
        x2 = z[r0 + half:r0 + ROT_DIM]
        parts += [x1 * cos - x2 * sin, x2 * cos + x1 * sin, z[r0 + ROT_DIM:r0 + HEAD_DIM]]
    return jnp.concatenate(parts, axis=0)


def _inproj_kernel(x_ref, mod_ref, g_ref, wt_ref, wc_ref, cos_ref, sin_ref,
                   qt_ref, k_ref, vt_ref, u_ref):
    x = x_ref[0]
    h = _modulated_rms_norm(x, g_ref[...], mod_ref[0, 1:2, :], mod_ref[0, 0:1, :]).astype(BF16)
    cos = cos_ref[0]
    sin = sin_ref[0]

    zq = lax.dot_general(wt_ref[0:ATTN_W, :], h, _NT_DIMS, preferred_element_type=F32)
    for hd in range(ATTN_HEADS):
        t = _rope_rows(zq[hd * HEAD_W:(hd + 1) * HEAD_W], cos, sin)
        qt_ref[0, hd, 0] = (t * Q_SCALE).astype(BF16)

    zk = lax.dot_general(wt_ref[ATTN_W:2 * ATTN_W, :], h, _NT_DIMS, preferred_element_type=F32)
    for hd in range(ATTN_HEADS):
        k_ref[0, hd] = _rope_rows(zk[hd * HEAD_W:(hd + 1) * HEAD_W], cos, sin).T.astype(BF16)

    zv = lax.dot_general(wt_ref[2 * ATTN_W:3 * ATTN_W, :], h, _NT_DIMS, preferred_element_type=F32)
    for hd in range(ATTN_HEADS):
        vt_ref[0, hd, 0, 0:HEAD_W, :] = zv[hd * HEAD_W:(hd + 1) * HEAD_W].astype(BF16)
        vt_ref[0, hd, 0, HEAD_W:V_ROWS, :] = jnp.ones((V_ROWS - HEAD_W, zv.shape[1]), BF16)

    ga = jnp.dot(h, wc_ref[:, 0:CONV_W], preferred_element_type=F32)
    gb = jnp.dot(h, wc_ref[:, CONV_W:2 * CONV_W], preferred_element_type=F32)
    u_ref[0] = ga * _sigmoid(gb)


def _inproj_call(x, mod, g, wt, wc, cos_t, sin_t):
    b, s, d = x.shape
    tm = min(ATTN_TILE, s)
    row = lambda bi, i: (bi, i, 0)
    t_spec = lambda rows: pl.BlockSpec((1, ATTN_HEADS, 1, rows, tm), lambda bi, i: (bi, 0, i, 0, 0))
    t_shape = lambda rows: jax.ShapeDtypeStruct((b, ATTN_HEADS, s // tm, rows, tm), BF16)
    rot_spec = pl.BlockSpec((1, ROT_DIM // 2, tm), lambda bi, i: (bi, 0, i))
    return pl.pallas_call(
        _inproj_kernel,
        grid=(b, s // tm),
        in_specs=[
            pl.BlockSpec((1, tm, d), row),
            pl.BlockSpec((1, N_MOD, d), lambda bi, i: (bi, 0, 0)),
            _resident((1, d)),
            _resident(wt.shape),
            _resident(wc.shape),
            rot_spec, rot_spec,
        ],
        out_specs=[
            t_spec(HEAD_W),
            pl.BlockSpec((1, ATTN_HEADS, tm, HEAD_W), lambda bi, i: (bi, 0, i, 0)),
            t_spec(V_ROWS),
            pl.BlockSpec((1, tm, CONV_W), row),
        ],
        out_shape=[
            t_shape(HEAD_W),
            jax.ShapeDtypeStruct((b, ATTN_HEADS, s, HEAD_W), BF16),
            t_shape(V_ROWS),
            jax.ShapeDtypeStruct((b, s, CONV_W), F32),
        ],
        compiler_params=pltpu.CompilerParams(
            dimension_semantics=("arbitrary", "arbitrary"),
            vmem_limit_bytes=V7X_VMEM_LIMIT_BYTES),
        name="in_proj",
    )(x, mod, g, wt, wc, cos_t, sin_t)


def _attn_kernel(qt_ref, k_ref, vt_ref, lq1_ref, lk1_ref, lq2_ref, lk2_ref, g_ref, o_ref,
                 qbd_scr, m_scr, acc_scr, s0, s1, x0, x1, *, tile, lambda_init):
    n_tiles = qt_ref.shape[2]
    groups = [slice(c0, c0 + ATTN_GROUP) for c0 in range(0, 2 * tile, ATTN_GROUP)]
    even = _StageBuffers(s0, x0)
    odd = _StageBuffers(s1, x1)

    def load_queries(i):
        qt = qt_ref[0, 0, i]
        first = lax.broadcasted_iota(jnp.int32, qt.shape, 0) < HEAD_DIM
        zero = jnp.zeros_like(qt)
        qbd_scr[:, 0:tile] = jnp.where(first, qt, zero)
        qbd_scr[:, tile:2 * tile] = jnp.where(first, zero, qt)

    def scores(j, buf, cols):
        start = pl.multiple_of(j * tile, tile)
        kb = k_ref[0, 0, pl.ds(start, tile), :]
        s = jnp.dot(kb, qbd_scr[:, cols], preferred_element_type=F32)
        buf.s[:, cols] = s
        buf.smax[:, cols] = jnp.max(s, axis=0, keepdims=True)

    def softmax(j, buf, cols, masked):
        if masked:
            first_query = cols.start % tile
            keys = min(tile, first_query + ATTN_GROUP)
            s = buf.s[0:keys, cols]
            key = lax.broadcasted_iota(jnp.int32, s.shape, 0)
            query = lax.broadcasted_iota(jnp.int32, s.shape, 1) + first_query
            s = jnp.where(key <= query, s, MASK_VALUE)
            block_max = jnp.max(s, axis=0, keepdims=True)
        else:
            keys = tile
            s = buf.s[:, cols]
            block_max = buf.smax[:, cols]
        m_prev = m_scr[:, cols]
        m_new = jnp.maximum(m_prev, block_max)
        p = jnp.exp2(s - m_new).astype(BF16)
        alpha = jnp.exp2(m_prev - m_new)
        m_scr[:, cols] = m_new
        acc_scr[:, cols] = alpha * acc_scr[:, cols] + jnp.dot(
            vt_ref[0, 0, j, :, 0:keys], p, preferred_element_type=F32)

    def substep(j, cur, other):
        for cols in groups:
            scores(j + 1, other, cols)
            softmax(j, cur, cols, False)

    def last_block(i, cur):
        for cols in groups:
            softmax(i, cur, cols, True)
            scores(0, even, cols)

    lam = (jnp.exp(jnp.sum(lq1_ref[...] * lk1_ref[...], axis=-1, keepdims=True))
           - jnp.exp(jnp.sum(lq2_ref[...] * lk2_ref[...], axis=-1, keepdims=True))
           + lambda_init)

    def finish(i):
        o1 = acc_scr[0:HEAD_W, 0:tile] / acc_scr[HEAD_W:HEAD_W + 1, 0:tile]
        o2 = acc_scr[0:HEAD_W, tile:2 * tile] / acc_scr[HEAD_W:HEAD_W + 1, tile:2 * tile]
        o = o1 - lam * o2
        o = o * lax.rsqrt(jnp.mean(o * o, axis=0, keepdims=True) + 1e-5)
        o = (o * g_ref[...]) * (1.0 - lambda_init)
        o_ref[0, pl.ds(pl.multiple_of(i * tile, tile), tile), :] = o.T.astype(BF16)

    def query_tile(i, carry):
        m_scr[...] = jnp.full(m_scr.shape, MASK_VALUE, F32)
        acc_scr[...] = jnp.zeros(acc_scr.shape, F32)
        i_next = jnp.minimum(i + 1, n_tiles - 1)

        def pair(jj, c):
            substep(2 * jj, even, odd)
            substep(2 * jj + 1, odd, even)
            return c

        lax.fori_loop(0, lax.shift_right_logical(i, 1), pair, 0)

        @pl.when((i & 1) == 1)
        def _():
            substep(i - 1, even, odd)
            load_queries(i_next)
            last_block(i, odd)

        @pl.when((i & 1) == 0)
        def _():
            load_queries(i_next)
            last_block(i, even)

        finish(i)
        return carry

    load_queries(0)
    for cols in groups:
        scores(0, even, cols)
    lax.fori_loop(0, n_tiles, query_tile, 0)


def _attn_call(qt, k, vt, lq1, lk1, lq2, lk2, subln_g_col, lambda_init):
    b, nh, s, _ = k.shape
    tile = qt.shape[-1]
    vec = lambda n: pl.BlockSpec((1, n), lambda bi, hi: (0, 0))
    return pl.pallas_call(
        functools.partial(_attn_kernel, tile=tile, lambda_init=lambda_init),
        grid=(b, nh),
        in_specs=[
            pl.BlockSpec((1, 1, s // tile, HEAD_W, tile), lambda bi, hi: (bi, hi, 0, 0, 0)),
            pl.BlockSpec((1, 1, s, HEAD_W), lambda bi, hi: (bi, hi, 0, 0)),
            pl.BlockSpec((1, 1, s // tile, V_ROWS, tile), lambda bi, hi: (bi, hi, 0, 0, 0)),
            vec(HEAD_DIM), vec(HEAD_DIM), vec(HEAD_DIM), vec(HEAD_DIM),
            pl.BlockSpec((HEAD_W, 1), lambda bi, hi: (0, 0)),
        ],
        out_specs=pl.BlockSpec((1, s, HEAD_W), lambda bi, hi: (bi, 0, hi)),
        out_shape=jax.ShapeDtypeStruct((b, s, nh * HEAD_W), BF16),
        scratch_shapes=[
            pltpu.VMEM((HEAD_W, 2 * tile), BF16),
            pltpu.VMEM((1, 2 * tile), F32),
            pltpu.VMEM((V_ROWS, 2 * tile), F32),
            pltpu.VMEM((tile, 2 * tile), F32), pltpu.VMEM((tile, 2 * tile), F32),
            pltpu.VMEM((1, 2 * tile), F32), pltpu.VMEM((1, 2 * tile), F32),
        ],
        compiler_params=pltpu.CompilerParams(
            dimension_semantics=("arbitrary", "arbitrary"),
            vmem_limit_bytes=V7X_VMEM_LIMIT_BYTES),
        name="diff_attn",
    )(qt, k, vt, lq1, lk1, lq2, lk2, subln_g_col)


def _mixer_kernel(x_ref, mod_ref, g_ref, wg_ref, o_ref, u_ref, uh_ref, cw_ref, cb_ref, lg_ref, lb_ref,
                  wao_ref, wco_ref, wout_ref, x1_ref, ubuf, *, tm):
    i = pl.program_id(1)
    d = x_ref.shape[-1]
    x = x_ref[0]
    h = _modulated_rms_norm(x, g_ref[...], mod_ref[0, 1:2, :], mod_ref[0, 0:1, :]).astype(BF16)
    halo = uh_ref[0]
    ubuf[0:CONV_HALO, :] = jnp.where(i > 0, halo, jnp.zeros_like(halo))
    ubuf[CONV_HALO:CONV_HALO + tm, :] = u_ref[0]
    ubuf[CONV_HALO + tm:CONV_HALO + tm + SUBLANES, :] = jnp.zeros((SUBLANES, CONV_W), F32)
    base = CONV_HALO - (CONV_K - 1)

    def conv_lanes(lanes):
        out = cb_ref[:, lanes]
        for r in range(SUBLANES):
            part = None
            for a in range((base + CONV_K - 1 - r) // SUBLANES + 1):
                j = SUBLANES * a + r - base
                if j < 0:
                    continue
                term = cw_ref[j:j + 1, lanes] * ubuf[SUBLANES * a:SUBLANES * a + tm + SUBLANES, lanes]
                part = term if part is None else part + term
            out = out + part[r:r + tm, :]
        return out

    n_chunks = CONV_W // LANES
    gate_cols = wg_ref.shape[1] // n_chunks
    gate_parts, conv_parts = [], []
    for c in range(n_chunks):
        z = jnp.dot(h, wg_ref[:, c * gate_cols:(c + 1) * gate_cols], preferred_element_type=F32)
        gate_parts.append(_sigmoid(z))
        conv_parts.append(conv_lanes(slice(c * LANES, (c + 1) * LANES)))
    gates = jnp.concatenate(gate_parts, axis=1)
    conv = jnp.concatenate(conv_parts, axis=1)

    mu = jnp.mean(conv, axis=-1, keepdims=True)
    cen = conv - mu
    var = jnp.mean(cen * cen, axis=-1, keepdims=True)
    y = (cen * lax.rsqrt(var + 1e-5)) * lg_ref[...] + lb_ref[...]
    act = (y * _sigmoid(y)).astype(BF16)

    y_conv = jnp.dot(act, wco_ref[...], preferred_element_type=F32)
    y_attn = jnp.dot(o_ref[0], wao_ref[...], preferred_element_type=F32)
    mix = (gates[:, 0:d] * y_attn + gates[:, d:2 * d] * y_conv).astype(BF16)
    mixed = jnp.dot(mix, wout_ref[...], preferred_element_type=F32)
    x1_ref[0] = x + (1.0 + mod_ref[0, 2:3, :]) * mixed


def _mixer_call(x, mod, g, wg, o, u, cw, cb, lg, lb, wao, wco, wout):
    b, s, d = x.shape
    tm = min(ROW_TILE, s)
    halo_blocks = tm // CONV_HALO
    row = lambda bi, i: (bi, i, 0)
    return pl.pallas_call(
        functools.partial(_mixer_kernel, tm=tm),
        grid=(b, s // tm),
        in_specs=[
            pl.BlockSpec((1, tm, d), row),
            pl.BlockSpec((1, N_MOD, d), lambda bi, i: (bi, 0, 0)),
            _resident((1, d)),
            _resident(wg.shape),
            pl.BlockSpec((1, tm, ATTN_W), row),
            pl.BlockSpec((1, tm, CONV_W), row),
            pl.BlockSpec((1, CONV_HALO, CONV_W), lambda bi, i: (bi, jnp.maximum(i * halo_blocks - 1, 0), 0)),
            _resident(cw.shape), _resident(cb.shape), _resident(lg.shape), _resident(lb.shape),
            _resident(wao.shape), _resident(wco.shape), _resident(wout.shape),
        ],
        out_specs=pl.BlockSpec((1, tm, d), row),
        out_shape=jax.ShapeDtypeStruct((b, s, d), F32),
        scratch_shapes=[pltpu.VMEM((CONV_HALO + tm + SUBLANES, CONV_W), F32)],
        compiler_params=pltpu.CompilerParams(
            dimension_semantics=("arbitrary", "arbitrary"),
            vmem_limit_bytes=V7X_VMEM_LIMIT_BYTES),
        name="mixer",
    )(x, mod, g, wg, o, u, u, cw, cb, lg, lb, wao, wco, wout)


def _ffn_kernel(x_ref, mod_ref, g_ref, wi_ref, wo_ref, fg_ref, out_ref, *, d_ff, final):
    x = x_ref[0]
    h = _modulated_rms_norm(x, g_ref[...], mod_ref[0, 4:5, :], mod_ref[0, 3:4, :]).astype(BF16)
    acc = jnp.zeros(x.shape, F32)
    for c0 in range(0, d_ff, FFN_CHUNK):
        cw = min(FFN_CHUNK, d_ff - c0)
        f_gate = jnp.dot(h, wi_ref[:, c0:c0 + cw], preferred_element_type=F32)
        f_up = jnp.dot(h, wi_ref[:, d_ff + c0:d_ff + c0 + cw], preferred_element_type=F32)
        a = ((f_gate * _sigmoid(f_gate)) * f_up).astype(BF16)
        acc = acc + jnp.dot(a, wo_ref[c0:c0 + cw, :], preferred_element_type=F32)
    x2 = x + (1.0 + mod_ref[0, 5:6, :]) * acc
    if final:
        x2 = (x2 * lax.rsqrt(jnp.mean(x2 * x2, axis=-1, keepdims=True) + 1e-6)) * fg_ref[...]
    out_ref[0] = x2


def _ffn_call(x, mod, g, wi, wo, final_g, final):
    b, s, d = x.shape
    d_ff = wo.shape[0]
    tm = min(ROW_TILE, s)
    row = lambda bi, i: (bi, i, 0)
    return pl.pallas_call(
        functools.partial(_ffn_kernel, d_ff=d_ff, final=final),
        grid=(b, s // tm),
        in_specs=[
            pl.BlockSpec((1, tm, d), row),
            pl.BlockSpec((1, N_MOD, d), lambda bi, i: (bi, 0, 0)),
            _resident((1, d)),
            _resident(wi.shape),
            _resident(wo.shape),
            _resident((1, d)),
        ],
        out_specs=pl.BlockSpec((1, tm, d), row),
        out_shape=jax.ShapeDtypeStruct((b, s, d), F32),
        compiler_params=pltpu.CompilerParams(
            dimension_semantics=("arbitrary", "arbitrary"),
            vmem_limit_bytes=V7X_VMEM_LIMIT_BYTES),
        name="ffn_final" if final else "ffn",
    )(x, mod, g, wi, wo, final_g)


def _rope_tables(positions):
    inv_freq = 1.0 / (ROPE_THETA ** (jnp.arange(0, ROT_DIM, 2, dtype=F32) / ROT_DIM))
    ang = positions.astype(F32)[..., None] * inv_freq
    return jnp.swapaxes(jnp.cos(ang), 1, 2), jnp.swapaxes(jnp.sin(ang), 1, 2)


def kernel(x, c, positions, ada_w, ada_b, norm1_g, w_in, lambda_q1, lambda_k1, lambda_q2, lambda_k2,
           subln_g, w_attn_o, dw_conv_w, dw_conv_b, conv_ln_g, conv_ln_b, w_conv_o, w_out, norm2_g,
           w_ffn_in, w_ffn_out, final_g):
    b, s, d = x.shape
    depth = ada_w.shape[0]
    assert s % min(ROW_TILE, s) == 0 and s % min(ATTN_TILE, s) == 0
    assert (2 * min(ATTN_TILE, s)) % ATTN_GROUP == 0
    assert ada_w.shape[2] == N_MOD * d and ada_w.shape[2] % ADA_COLS == 0
    assert w_in.shape[2] == N_QKVU + 2 * d

    rows = -(-b // 8) * 8
    c_pad = jnp.zeros((rows, d), F32).at[:b].set(c)
    mod_all = _ada_call(c_pad, ada_w, ada_b)[:, :b].reshape(depth, b, N_MOD, d)

    cos_t, sin_t = _rope_tables(positions)
    final_row = final_g.reshape(1, d)

    for l in range(depth):
        lambda_init = 0.8 - 0.6 * math.exp(-0.3 * l)
        mod = mod_all[l]
        g1 = norm1_g[l].reshape(1, d)
        w_l = w_in[l]
        qt, k, vt, u = _inproj_call(x, mod, g1, w_l[:, :3 * ATTN_W].T.astype(BF16),
                                    w_l[:, 3 * ATTN_W:N_QKVU].astype(BF16), cos_t, sin_t)
        o = _attn_call(qt, k, vt,
                       lambda_q1[l].reshape(1, HEAD_DIM), lambda_k1[l].reshape(1, HEAD_DIM),
                       lambda_q2[l].reshape(1, HEAD_DIM), lambda_k2[l].reshape(1, HEAD_DIM),
                       subln_g[l].reshape(HEAD_W, 1), lambda_init)
        x = _mixer_call(x, mod, g1, w_l[:, N_QKVU:].astype(BF16), o, u,
                        dw_conv_w[l].reshape(CONV_K, CONV_W), dw_conv_b[l].reshape(1, CONV_W),
                        conv_ln_g[l].reshape(1, CONV_W), conv_ln_b[l].reshape(1, CONV_W),
                        w_attn_o[l].astype(BF16), w_conv_o[l].astype(BF16), w_out[l].astype(BF16))
        x = _ffn_call(x, mod, norm2_g[l].reshape(1, d), w_ffn_in[l].astype(BF16),
                      w_ffn_out[l].astype(BF16), final_row, final=(l == depth - 1))
    return x
```

```python
import collections
import functools
import math

import jax
import jax.numpy as jnp
from jax import lax
from jax.experimental import pallas as pl
from jax.experimental.pallas import tpu as pltpu

F32 = jnp.float32
BF16 = jnp.bfloat16

ATTN_HEADS = 4
HEAD_DIM = 64
HEAD_W = 2 * HEAD_DIM
ATTN_W = ATTN_HEADS * HEAD_W
ROT_DIM = HEAD_DIM // 4
ROPE_THETA = 500000.0
CONV_W = 512
CONV_K = 31
N_MOD = 6
N_QKVU = 3 * ATTN_W + 2 * CONV_W
CONV_HALO = 32
MASK_VALUE = -1e30
Q_SCALE = math.log2(math.e) / math.sqrt(HEAD_DIM)
V_ROWS = HEAD_W + 16

V7X_VMEM_LIMIT_BYTES = 56 * 1024 * 1024
V7X_MXU_COLS = 256
SUBLANES = 8
LANES = 128

ROW_TILE = 512
FFN_ROW_TILE = 1024
ATTN_TILE = 512
ATTN_GROUP = V7X_MXU_COLS
ADA_COLS = 1024
FFN_CHUNK = 1024


def _sigmoid(x):
    return 0.5 * jnp.tanh(0.5 * x) + 0.5


def _modulated_rms_norm(x, g, scale, shift, eps=1e-6):
    y = x * lax.rsqrt(jnp.mean(x * x, axis=-1, keepdims=True) + eps)
    return (y * g) * (1.0 + scale) + shift


def _resident(shape):
    zeros = (0,) * len(shape)
    return pl.BlockSpec(shape, lambda *_: zeros, pipeline_mode=pl.Buffered(1))


def _ada_kernel(c_ref, w_ref, b_ref, o_ref):
    c = c_ref[...]
    c_act = (c * _sigmoid(c)).astype(BF16)
    w = w_ref[0].astype(BF16)
    o_ref[0] = jnp.dot(c_act, w, preferred_element_type=F32) + b_ref[0]


def _ada_call(c_pad, ada_w, ada_b):
    depth, d, n = ada_w.shape
    rows = c_pad.shape[0]
    return pl.pallas_call(
        _ada_kernel,
        grid=(depth, n // ADA_COLS),
        in_specs=[
            pl.BlockSpec((rows, d), lambda l, j: (0, 0)),
            pl.BlockSpec((1, d, ADA_COLS), lambda l, j: (l, 0, j)),
            pl.BlockSpec((1, 1, ADA_COLS), lambda l, j: (l, 0, j)),
        ],
        out_specs=pl.BlockSpec((1, rows, ADA_COLS), lambda l, j: (l, 0, j)),
        out_shape=jax.ShapeDtypeStruct((depth, rows, n), F32),
        compiler_params=pltpu.CompilerParams(
            dimension_semantics=("arbitrary", "arbitrary"),
            vmem_limit_bytes=V7X_VMEM_LIMIT_BYTES),
        name="ada_proj",
    )(c_pad, ada_w, ada_b.reshape(depth, 1, n))


_NT_DIMS = (((1,), (1,)), ((), ()))

_StageBuffers = collections.namedtuple("_StageBuffers", ["s", "smax"])


def _rope_rows(z, cos, sin):
    half = ROT_DIM // 2
    parts = []
    for comp in range(2):
        r0 = comp * HEAD_DIM
        x1 = z[r0:r0 + half]
        x2 = z[r0 + half:r0 + ROT_DIM]
        parts += [x1 * cos - x2 * sin, x2 * cos + x1 * sin, z[r0 + ROT_DIM:r0 + HEAD_DIM]]
    return jnp.concatenate(parts, axis=0)


def _inproj_kernel(x_ref, mod_ref, g_ref, wt_ref, wc_ref, cos_ref, sin_ref,
                   qt_ref, k_ref, vt_ref, u_ref, h_ref):
    x = x_ref[0]
    h = _modulated_rms_norm(x, g_ref[...], mod_ref[0, 1:2, :], mod_ref[0, 0:1, :]).astype(BF16)
    h_ref[0] = h
    cos = cos_ref[0]
    sin = sin_ref[0]

    zq = lax.dot_general(wt_ref[0:ATTN_W, :], h, _NT_DIMS, preferred_element_type=F32)
    for hd in range(ATTN_HEADS):
        t = _rope_rows(zq[hd * HEAD_W:(hd + 1) * HEAD_W], cos, sin)
        qt_ref[0, hd, 0] = (t * Q_SCALE).astype(BF16)

    zk = lax.dot_general(wt_ref[ATTN_W:2 * ATTN_W, :], h, _NT_DIMS, preferred_element_type=F32)
    for hd in range(ATTN_HEADS):
        k_ref[0, hd] = _rope_rows(zk[hd * HEAD_W:(hd + 1) * HEAD_W], cos, sin).T.astype(BF16)

    zv = lax.dot_general(wt_ref[2 * ATTN_W:3 * ATTN_W, :], h, _NT_DIMS, preferred_element_type=F32)
    for hd in range(ATTN_HEADS):
        vt_ref[0, hd, 0, 0:HEAD_W, :] = zv[hd * HEAD_W:(hd + 1) * HEAD_W].astype(BF16)
        vt_ref[0, hd, 0, HEAD_W:V_ROWS, :] = jnp.ones((V_ROWS - HEAD_W, zv.shape[1]), BF16)

    ga = jnp.dot(h, wc_ref[:, 0:CONV_W], preferred_element_type=F32)
    gb = jnp.dot(h, wc_ref[:, CONV_W:2 * CONV_W], preferred_element_type=F32)
    u_ref[0] = ga * _sigmoid(gb)


def _inproj_call(x, mod, g, wt, wc, cos_t, sin_t):
    b, s, d = x.shape
    tm = min(ATTN_TILE, s)
    row = lambda bi, i: (bi, i, 0)
    t_spec = lambda rows: pl.BlockSpec((1, ATTN_HEADS, 1, rows, tm), lambda bi, i: (bi, 0, i, 0, 0))
    t_shape = lambda rows: jax.ShapeDtypeStruct((b, ATTN_HEADS, s // tm, rows, tm), BF16)
    rot_spec = pl.BlockSpec((1, ROT_DIM // 2, tm), lambda bi, i: (bi, 0, i))
    return pl.pallas_call(
        _inproj_kernel,
        grid=(b, s // tm),
        in_specs=[
            pl.BlockSpec((1, tm, d), row),
            pl.BlockSpec((1, N_MOD, d), lambda bi, i: (bi, 0, 0)),
            _resident((1, d)),
            _resident(wt.shape),
            _resident(wc.shape),
            rot_spec, rot_spec,
        ],
        out_specs=[
            t_spec(HEAD_W),
            pl.BlockSpec((1, ATTN_HEADS, tm, HEAD_W), lambda bi, i: (bi, 0, i, 0)),
            t_spec(V_ROWS),
            pl.BlockSpec((1, tm, CONV_W), row),
            pl.BlockSpec((1, tm, d), row),
        ],
        out_shape=[
            t_shape(HEAD_W),
            jax.ShapeDtypeStruct((b, ATTN_HEADS, s, HEAD_W), BF16),
            t_shape(V_ROWS),
            jax.ShapeDtypeStruct((b, s, CONV_W), F32),
            jax.ShapeDtypeStruct((b, s, d), BF16),
        ],
        compiler_params=pltpu.CompilerParams(
            dimension_semantics=("arbitrary", "arbitrary"),
            vmem_limit_bytes=V7X_VMEM_LIMIT_BYTES),
        name="in_proj",
    )(x, mod, g, wt, wc, cos_t, sin_t)


def _attn_kernel(qt_ref, k_ref, vt_ref, lq1_ref, lk1_ref, lq2_ref, lk2_ref, g_ref, o_ref,
                 qbd_scr, m_scr, acc_scr, s0, s1, x0, x1, *, tile, lambda_init):
    n_tiles = qt_ref.shape[2]
    groups = [slice(c0, c0 + ATTN_GROUP) for c0 in range(0, 2 * tile, ATTN_GROUP)]
    even = _StageBuffers(s0, x0)
    odd = _StageBuffers(s1, x1)

    def load_queries(i):
        qt = qt_ref[0, 0, i]
        first = lax.broadcasted_iota(jnp.int32, qt.shape, 0) < HEAD_DIM
        zero = jnp.zeros_like(qt)
        qbd_scr[:, 0:tile] = jnp.where(first, qt, zero)
        qbd_scr[:, tile:2 * tile] = jnp.where(first, zero, qt)

    def read_cols(ref, cols, rows=slice(None)):
        return jnp.concatenate([ref[t, rows, :] for t in range(cols.start // LANES, cols.stop // LANES)], axis=1)

    def write_cols(ref, cols, value):
        for n, t in enumerate(range(cols.start // LANES, cols.stop // LANES)):
            ref[t] = value[:, n * LANES:(n + 1) * LANES]

    def scores(j, buf, cols):
        start = pl.multiple_of(j * tile, tile)
        kb = k_ref[0, 0, pl.ds(start, tile), :]
        s = jnp.dot(kb, qbd_scr[:, cols], preferred_element_type=F32)
        write_cols(buf.s, cols, s)
        buf.smax[:, cols] = jnp.max(s, axis=0, keepdims=True)

    def softmax(j, buf, cols, masked):
        if masked:
            first_query = cols.start % tile
            keys = min(tile, first_query + ATTN_GROUP)
            s = read_cols(buf.s, cols, slice(0, keys))
            key = lax.broadcasted_iota(jnp.int32, s.shape, 0)
            query = lax.broadcasted_iota(jnp.int32, s.shape, 1) + first_query
            s = jnp.where(key <= query, s, MASK_VALUE)
            block_max = jnp.max(s, axis=0, keepdims=True)
        else:
            keys = tile
            s = read_cols(buf.s, cols)
            block_max = buf.smax[:, cols]
        m_prev = m_scr[:, cols]
        m_new = jnp.maximum(m_prev, block_max)
        p = jnp.exp2(s - m_new).astype(BF16)
        alpha = jnp.exp2(m_prev - m_new)
        m_scr[:, cols] = m_new
        write_cols(acc_scr, cols, alpha * read_cols(acc_scr, cols) + jnp.dot(
            vt_ref[0, 0, j, :, 0:keys], p, preferred_element_type=F32))

    def substep(j, cur, other):
        for cols in groups:
            scores(j + 1, other, cols)
            softmax(j, cur, cols, False)

    def last_block(i, cur):
        for cols in groups:
            softmax(i, cur, cols, True)
            scores(0, even, cols)

    lam = (jnp.exp(jnp.sum(lq1_ref[...] * lk1_ref[...], axis=-1, keepdims=True))
           - jnp.exp(jnp.sum(lq2_ref[...] * lk2_ref[...], axis=-1, keepdims=True))
           + lambda_init)

    def finish(i):
        acc1 = read_cols(acc_scr, slice(0, tile))
        acc2 = read_cols(acc_scr, slice(tile, 2 * tile))
        o1 = acc1[0:HEAD_W] / acc1[HEAD_W:HEAD_W + 1]
        o2 = acc2[0:HEAD_W] / acc2[HEAD_W:HEAD_W + 1]
        o = o1 - lam * o2
        o = o * lax.rsqrt(jnp.mean(o * o, axis=0, keepdims=True) + 1e-5)
        o = (o * g_ref[...]) * (1.0 - lambda_init)
        o_ref[0, pl.ds(pl.multiple_of(i * tile, tile), tile), :] = o.T.astype(BF16)

    def query_tile(i, carry):
        m_scr[...] = jnp.full(m_scr.shape, MASK_VALUE, F32)
        acc_scr[...] = jnp.zeros(acc_scr.shape, F32)
        i_next = jnp.minimum(i + 1, n_tiles - 1)

        def pair(jj, c):
            substep(2 * jj, even, odd)
            substep(2 * jj + 1, odd, even)
            return c

        lax.fori_loop(0, lax.shift_right_logical(i, 1), pair, 0)

        @pl.when((i & 1) == 1)
        def _():
            substep(i - 1, even, odd)
            load_queries(i_next)
            last_block(i, odd)

        @pl.when((i & 1) == 0)
        def _():
            load_queries(i_next)
            last_block(i, even)

        finish(i)
        return carry

    load_queries(0)
    for cols in groups:
        scores(0, even, cols)
    lax.fori_loop(0, n_tiles, query_tile, 0)


def _attn_call(qt, k, vt, lq1, lk1, lq2, lk2, subln_g_col, lambda_init):
    b, nh, s, _ = k.shape
    tile = qt.shape[-1]
    vec = lambda n: pl.BlockSpec((1, n), lambda bi, hi: (0, 0))
    return pl.pallas_call(
        functools.partial(_attn_kernel, tile=tile, lambda_init=lambda_init),
        grid=(b, nh),
        in_specs=[
            pl.BlockSpec((1, 1, s // tile, HEAD_W, tile), lambda bi, hi: (bi, hi, 0, 0, 0)),
            pl.BlockSpec((1, 1, s, HEAD_W), lambda bi, hi: (bi, hi, 0, 0)),
            pl.BlockSpec((1, 1, s // tile, V_ROWS, tile), lambda bi, hi: (bi, hi, 0, 0, 0)),
            vec(HEAD_DIM), vec(HEAD_DIM), vec(HEAD_DIM), vec(HEAD_DIM),
            pl.BlockSpec((HEAD_W, 1), lambda bi, hi: (0, 0)),
        ],
        out_specs=pl.BlockSpec((1, s, HEAD_W), lambda bi, hi: (bi, 0, hi)),
        out_shape=jax.ShapeDtypeStruct((b, s, nh * HEAD_W), BF16),
        scratch_shapes=[
            pltpu.VMEM((HEAD_W, 2 * tile), BF16),
            pltpu.VMEM((1, 2 * tile), F32),
            pltpu.VMEM((2 * tile // LANES, V_ROWS, LANES), F32),
            pltpu.VMEM((2 * tile // LANES, tile, LANES), F32),
            pltpu.VMEM((2 * tile // LANES, tile, LANES), F32),
            pltpu.VMEM((1, 2 * tile), F32), pltpu.VMEM((1, 2 * tile), F32),
        ],
        compiler_params=pltpu.CompilerParams(
            dimension_semantics=("arbitrary", "arbitrary"),
            vmem_limit_bytes=V7X_VMEM_LIMIT_BYTES),
        name="diff_attn",
    )(qt, k, vt, lq1, lk1, lq2, lk2, subln_g_col)


def _mixer_kernel(x_ref, mod_ref, h_ref, wg_ref, o_ref, u_ref, uh_ref, cw_ref, cb_ref, lg_ref, lb_ref,
                  wao_ref, wco_ref, wout_ref, x1_ref, ubuf, *, tm):
    i = pl.program_id(1)
    d = x_ref.shape[-1]
    x = x_ref[0]
    h = h_ref[0]
    halo = uh_ref[0]
    ubuf[0:CONV_HALO, :] = jnp.where(i > 0, halo, jnp.zeros_like(halo))
    ubuf[CONV_HALO:CONV_HALO + tm, :] = u_ref[0]
    ubuf[CONV_HALO + tm:CONV_HALO + tm + SUBLANES, :] = jnp.zeros((SUBLANES, CONV_W), F32)
    base = CONV_HALO - (CONV_K - 1)

    def conv_lanes(lanes):
        out = cb_ref[:, lanes]
        for r in range(SUBLANES):
            part = None
            for a in range((base + CONV_K - 1 - r) // SUBLANES + 1):
                j = SUBLANES * a + r - base
                if j < 0:
                    continue
                term = cw_ref[j:j + 1, lanes] * ubuf[SUBLANES * a:SUBLANES * a + tm + SUBLANES, lanes]
                part = term if part is None else part + term
            out = out + part[r:r + tm, :]
        return out

    n_chunks = CONV_W // LANES
    gate_cols = wg_ref.shape[1] // n_chunks
    gate_parts, conv_parts = [], []
    for c in range(n_chunks):
        z = jnp.dot(h, wg_ref[:, c * gate_cols:(c + 1) * gate_cols], preferred_element_type=F32)
        gate_parts.append(_sigmoid(z))
        conv_parts.append(conv_lanes(slice(c * LANES, (c + 1) * LANES)))
    gates = jnp.concatenate(gate_parts, axis=1)
    conv = jnp.concatenate(conv_parts, axis=1)

    mu = jnp.mean(conv, axis=-1, keepdims=True)
    cen = conv - mu
    var = jnp.mean(cen * cen, axis=-1, keepdims=True)
    y = (cen * lax.rsqrt(var + 1e-5)) * lg_ref[...] + lb_ref[...]
    act = (y * _sigmoid(y)).astype(BF16)

    y_conv = jnp.dot(act, wco_ref[...], preferred_element_type=F32)
    y_attn = jnp.dot(o_ref[0], wao_ref[...], preferred_element_type=F32)
    mix = (gates[:, 0:d] * y_attn + gates[:, d:2 * d] * y_conv).astype(BF16)
    mixed = jnp.dot(mix, wout_ref[...], preferred_element_type=F32)
    x1_ref[0] = x + (1.0 + mod_ref[0, 2:3, :]) * mixed


def _mixer_call(x, mod, h, wg, o, u, cw, cb, lg, lb, wao, wco, wout):
    b, s, d = x.shape
    tm = min(ROW_TILE, s)
    halo_blocks = tm // CONV_HALO
    row = lambda bi, i: (bi, i, 0)
    return pl.pallas_call(
        functools.partial(_mixer_kernel, tm=tm),
        grid=(b, s // tm),
        in_specs=[
            pl.BlockSpec((1, tm, d), row),
            pl.BlockSpec((1, N_MOD, d), lambda bi, i: (bi, 0, 0)),
            pl.BlockSpec((1, tm, d), row),
            _resident(wg.shape),
            pl.BlockSpec((1, tm, ATTN_W), row),
            pl.BlockSpec((1, tm, CONV_W), row),
            pl.BlockSpec((1, CONV_HALO, CONV_W), lambda bi, i: (bi, jnp.maximum(i * halo_blocks - 1, 0), 0)),
            _resident(cw.shape), _resident(cb.shape), _resident(lg.shape), _resident(lb.shape),
            _resident(wao.shape), _resident(wco.shape), _resident(wout.shape),
        ],
        out_specs=pl.BlockSpec((1, tm, d), row),
        out_shape=jax.ShapeDtypeStruct((b, s, d), F32),
        scratch_shapes=[pltpu.VMEM((CONV_HALO + tm + SUBLANES, CONV_W), F32)],
        compiler_params=pltpu.CompilerParams(
            dimension_semantics=("arbitrary", "arbitrary"),
            vmem_limit_bytes=V7X_VMEM_LIMIT_BYTES),
        name="mixer",
    )(x, mod, h, wg, o, u, u, cw, cb, lg, lb, wao, wco, wout)


def _ffn_kernel(x_ref, mod_ref, g_ref, wi_ref, wo_ref, fg_ref, out_ref, *, d_ff, final):
    x = x_ref[0]
    h = _modulated_rms_norm(x, g_ref[...], mod_ref[0, 4:5, :], mod_ref[0, 3:4, :]).astype(BF16)
    acc = jnp.zeros(x.shape, F32)
    for c0 in range(0, d_ff, FFN_CHUNK):
        cw = min(FFN_CHUNK, d_ff - c0)
        f_gate = jnp.dot(h, wi_ref[:, c0:c0 + cw], preferred_element_type=F32)
        f_up = jnp.dot(h, wi_ref[:, d_ff + c0:d_ff + c0 + cw], preferred_element_type=F32)
        a = ((f_gate * _sigmoid(f_gate)) * f_up).astype(BF16)
        acc = acc + jnp.dot(a, wo_ref[c0:c0 + cw, :], preferred_element_type=F32)
    x2 = x + (1.0 + mod_ref[0, 5:6, :]) * acc
    if final:
        x2 = (x2 * lax.rsqrt(jnp.mean(x2 * x2, axis=-1, keepdims=True) + 1e-6)) * fg_ref[...]
    out_ref[0] = x2


def _ffn_call(x, mod, g, wi, wo, final_g, final):
    b, s, d = x.shape
    d_ff = wo.shape[0]
    tm = min(FFN_ROW_TILE, s)
    row = lambda bi, i: (bi, i, 0)
    return pl.pallas_call(
        functools.partial(_ffn_kernel, d_ff=d_ff, final=final),
        grid=(b, s // tm),
        in_specs=[
            pl.BlockSpec((1, tm, d), row),
            pl.BlockSpec((1, N_MOD, d), lambda bi, i: (bi, 0, 0)),
            _resident((1, d)),
            _resident(wi.shape),
            _resident(wo.shape),
            _resident((1, d)),
        ],
        out_specs=pl.BlockSpec((1, tm, d), row),
        out_shape=jax.ShapeDtypeStruct((b, s, d), F32),
        compiler_params=pltpu.CompilerParams(
            dimension_semantics=("arbitrary", "arbitrary"),
            vmem_limit_bytes=V7X_VMEM_LIMIT_BYTES),
        name="ffn_final" if final else "ffn",
    )(x, mod, g, wi, wo, final_g)


def _rope_tables(positions):
    inv_freq = 1.0 / (ROPE_THETA ** (jnp.arange(0, ROT_DIM, 2, dtype=F32) / ROT_DIM))
    ang = positions.astype(F32)[..., None] * inv_freq
    return jnp.swapaxes(jnp.cos(ang), 1, 2), jnp.swapaxes(jnp.sin(ang), 1, 2)


def kernel(x, c, positions, ada_w, ada_b, norm1_g, w_in, lambda_q1, lambda_k1, lambda_q2, lambda_k2,
           subln_g, w_attn_o, dw_conv_w, dw_conv_b, conv_ln_g, conv_ln_b, w_conv_o, w_out, norm2_g,
           w_ffn_in, w_ffn_out, final_g):
    b, s, d = x.shape
    depth = ada_w.shape[0]
    assert all(s % min(t, s) == 0 for t in (ROW_TILE, FFN_ROW_TILE, ATTN_TILE))
    assert (2 * min(ATTN_TILE, s)) % ATTN_GROUP == 0
    assert ada_w.shape[2] == N_MOD * d and ada_w.shape[2] % ADA_COLS == 0
    assert w_in.shape[2] == N_QKVU + 2 * d

    rows = -(-b // 8) * 8
    c_pad = jnp.zeros((rows, d), F32).at[:b].set(c)
    mod_all = _ada_call(c_pad, ada_w, ada_b)[:, :b].reshape(depth, b, N_MOD, d)

    cos_t, sin_t = _rope_tables(positions)
    final_row = final_g.reshape(1, d)

    for l in range(depth):
        lambda_init = 0.8 - 0.6 * math.exp(-0.3 * l)
        mod = mod_all[l]
        g1 = norm1_g[l].reshape(1, d)
        w_l = w_in[l]
        qt, k, vt, u, h = _inproj_call(x, mod, g1, w_l[:, :3 * ATTN_W].T.astype(BF16),
                                    w_l[:, 3 * ATTN_W:N_QKVU].astype(BF16), cos_t, sin_t)
        o = _attn_call(qt, k, vt,
                       lambda_q1[l].reshape(1, HEAD_DIM), lambda_k1[l].reshape(1, HEAD_DIM),
                       lambda_q2[l].reshape(1, HEAD_DIM), lambda_k2[l].reshape(1, HEAD_DIM),
                       subln_g[l].reshape(HEAD_W, 1), lambda_init)
        x = _mixer_call(x, mod, h, w_l[:, N_QKVU:].astype(BF16), o, u,
                        dw_conv_w[l].reshape(CONV_K, CONV_W), dw_conv_b[l].reshape(1, CONV_W),
                        conv_ln_g[l].reshape(1, CONV_W), conv_ln_b[l].reshape(1, CONV_W),
                        w_attn_o[l].astype(BF16), w_conv_o[l].astype(BF16), w_out[l].astype(BF16))
        x = _ffn_call(x, mod, norm2_g[l].reshape(1, d), w_ffn_in[l].astype(BF16),
                      w_ffn_out[l].astype(BF16), final_row, final=(l == depth - 1))
    return x
```

```python
import collections
import functools
import math

import jax
import jax.numpy as jnp
from jax import lax
from jax.experimental import pallas as pl
from jax.experimental.pallas import tpu as pltpu

F32 = jnp.float32
BF16 = jnp.bfloat16

ATTN_HEADS = 4
HEAD_DIM = 64
HEAD_W = 2 * HEAD_DIM
ATTN_W = ATTN_HEADS * HEAD_W
ROT_DIM = HEAD_DIM // 4
ROPE_THETA = 500000.0
CONV_W = 512
CONV_K = 31
N_MOD = 6
N_QKVU = 3 * ATTN_W + 2 * CONV_W
CONV_HALO = 32
MASK_VALUE = -1e30
Q_SCALE = math.log2(math.e) / math.sqrt(HEAD_DIM)
V_ROWS = HEAD_W + 16

V7X_VMEM_LIMIT_BYTES = 56 * 1024 * 1024
V7X_MXU_COLS = 256
SUBLANES = 8
LANES = 128

ROW_TILE = 512
FFN_ROW_TILE = 1024
ATTN_TILE = 512
ATTN_GROUP = V7X_MXU_COLS
ADA_COLS = 1024
FFN_CHUNK = 1024


def _sigmoid(x):
    return 0.5 * jnp.tanh(0.5 * x) + 0.5


def _modulated_rms_norm(x, g, scale, shift, eps=1e-6):
    y = x * lax.rsqrt(jnp.mean(x * x, axis=-1, keepdims=True) + eps)
    return (y * g) * (1.0 + scale) + shift


def _resident(shape):
    zeros = (0,) * len(shape)
    return pl.BlockSpec(shape, lambda *_: zeros, pipeline_mode=pl.Buffered(1))


def _ada_kernel(c_ref, w_ref, b_ref, o_ref):
    c = c_ref[...]
    c_act = (c * _sigmoid(c)).astype(BF16)
    w = w_ref[0].astype(BF16)
    o_ref[0] = jnp.dot(c_act, w, preferred_element_type=F32) + b_ref[0]


def _ada_call(c_pad, ada_w, ada_b):
    depth, d, n = ada_w.shape
    rows = c_pad.shape[0]
    return pl.pallas_call(
        _ada_kernel,
        grid=(depth, n // ADA_COLS),
        in_specs=[
            pl.BlockSpec((rows, d), lambda l, j: (0, 0)),
            pl.BlockSpec((1, d, ADA_COLS), lambda l, j: (l, 0, j)),
            pl.BlockSpec((1, 1, ADA_COLS), lambda l, j: (l, 0, j)),
        ],
        out_specs=pl.BlockSpec((1, rows, ADA_COLS), lambda l, j: (l, 0, j)),
        out_shape=jax.ShapeDtypeStruct((depth, rows, n), F32),
        compiler_params=pltpu.CompilerParams(
            dimension_semantics=("arbitrary", "arbitrary"),
            vmem_limit_bytes=V7X_VMEM_LIMIT_BYTES),
        name="ada_proj",
    )(c_pad, ada_w, ada_b.reshape(depth, 1, n))


_NT_DIMS = (((1,), (1,)), ((), ()))

_StageBuffers = collections.namedtuple("_StageBuffers", ["s", "smax"])


def _rope_rows(z, cos, sin):
    half = ROT_DIM // 2
    parts = []
    for comp in range(2):
        r0 = comp * HEAD_DIM
        x1 = z[r0:r0 + half]
        x2 = z[r0 + half:r0 + ROT_DIM]
        parts += [x1 * cos - x2 * sin, x2 * cos + x1 * sin, z[r0 + ROT_DIM:r0 + HEAD_DIM]]
    return jnp.concatenate(parts, axis=0)


def _inproj_kernel(x_ref, mod_ref, g_ref, wt_ref, wc_ref, cos_ref, sin_ref,
                   qt_ref, k_ref, vt_ref, u_ref, h_ref):
    x = x_ref[0]
    h = _modulated_rms_norm(x, g_ref[...], mod_ref[0, 1:2, :], mod_ref[0, 0:1, :]).astype(BF16)
    h_ref[0] = h
    cos = cos_ref[0]
    sin = sin_ref[0]

    zq = lax.dot_general(wt_ref[0:ATTN_W, :], h, _NT_DIMS, preferred_element_type=F32)
    for hd in range(ATTN_HEADS):
        t = _rope_rows(zq[hd * HEAD_W:(hd + 1) * HEAD_W], cos, sin)
        qt_ref[0, hd, 0] = (t * Q_SCALE).astype(BF16)

    zk = lax.dot_general(wt_ref[ATTN_W:2 * ATTN_W, :], h, _NT_DIMS, preferred_element_type=F32)
    for hd in range(ATTN_HEADS):
        k_ref[0, hd] = _rope_rows(zk[hd * HEAD_W:(hd + 1) * HEAD_W], cos, sin).T.astype(BF16)

    zv = lax.dot_general(wt_ref[2 * ATTN_W:3 * ATTN_W, :], h, _NT_DIMS, preferred_element_type=F32)
    for hd in range(ATTN_HEADS):
        vt_ref[0, hd, 0, 0:HEAD_W, :] = zv[hd * HEAD_W:(hd + 1) * HEAD_W].astype(BF16)
        vt_ref[0, hd, 0, HEAD_W:V_ROWS, :] = jnp.ones((V_ROWS - HEAD_W, zv.shape[1]), BF16)

    ga = jnp.dot(h, wc_ref[:, 0:CONV_W], preferred_element_type=F32)
    gb = jnp.dot(h, wc_ref[:, CONV_W:2 * CONV_W], preferred_element_type=F32)
    u_ref[0] = ga * _sigmoid(gb)


def _inproj_call(x, mod, g, wt, wc, cos_t, sin_t):
    b, s, d = x.shape
    tm = min(ATTN_TILE, s)
    row = lambda bi, i: (bi, i, 0)
    t_spec = lambda rows: pl.BlockSpec((1, ATTN_HEADS, 1, rows, tm), lambda bi, i: (bi, 0, i, 0, 0))
    t_shape = lambda rows: jax.ShapeDtypeStruct((b, ATTN_HEADS, s // tm, rows, tm), BF16)
    rot_spec = pl.BlockSpec((1, ROT_DIM // 2, tm), lambda bi, i: (bi, 0, i))
    return pl.pallas_call(
        _inproj_kernel,
        grid=(b, s // tm),
        in_specs=[
            pl.BlockSpec((1, tm, d), row),
            pl.BlockSpec((1, N_MOD, d), lambda bi, i: (bi, 0, 0)),
            _resident((1, d)),
            _resident(wt.shape),
            _resident(wc.shape),
            rot_spec, rot_spec,
        ],
        out_specs=[
            t_spec(HEAD_W),
            pl.BlockSpec((1, ATTN_HEADS, tm, HEAD_W), lambda bi, i: (bi, 0, i, 0)),
            t_spec(V_ROWS),
            pl.BlockSpec((1, tm, CONV_W), row),
            pl.BlockSpec((1, tm, d), row),
        ],
        out_shape=[
            t_shape(HEAD_W),
            jax.ShapeDtypeStruct((b, ATTN_HEADS, s, HEAD_W), BF16),
            t_shape(V_ROWS),
            jax.ShapeDtypeStruct((b, s, CONV_W), F32),
            jax.ShapeDtypeStruct((b, s, d), BF16),
        ],
        compiler_params=pltpu.CompilerParams(
            dimension_semantics=("arbitrary", "arbitrary"),
            vmem_limit_bytes=V7X_VMEM_LIMIT_BYTES),
        name="in_proj",
    )(x, mod, g, wt, wc, cos_t, sin_t)


def _attn_kernel(qt_ref, k_ref, vt_ref, lq1_ref, lk1_ref, lq2_ref, lk2_ref, g_ref, o_ref,
                 qbd_scr, m_scr, acc_scr, s0, s1, x0, x1, *, tile, lambda_init):
    n_tiles = qt_ref.shape[2]
    groups = [slice(c0, c0 + ATTN_GROUP) for c0 in range(0, 2 * tile, ATTN_GROUP)]
    even = _StageBuffers(s0, x0)
    odd = _StageBuffers(s1, x1)

    def load_queries(i):
        qt = qt_ref[0, 0, i]
        first = lax.broadcasted_iota(jnp.int32, qt.shape, 0) < HEAD_DIM
        zero = jnp.zeros_like(qt)
        qbd_scr[:, 0:tile] = jnp.where(first, qt, zero)
        qbd_scr[:, tile:2 * tile] = jnp.where(first, zero, qt)

    def read_cols(ref, cols, rows=slice(None)):
        return jnp.concatenate([ref[t, rows, :] for t in range(cols.start // LANES, cols.stop // LANES)], axis=1)

    def write_cols(ref, cols, value):
        for n, t in enumerate(range(cols.start // LANES, cols.stop // LANES)):
            ref[t] = value[:, n * LANES:(n + 1) * LANES]

    def scores(j, buf, cols):
        start = pl.multiple_of(j * tile, tile)
        kb = k_ref[0, 0, pl.ds(start, tile), :]
        s = jnp.dot(kb, qbd_scr[:, cols], preferred_element_type=F32)
        write_cols(buf.s, cols, s)
        buf.smax[:, cols] = jnp.max(s, axis=0, keepdims=True)

    def softmax(j, buf, cols, masked):
        if masked:
            first_query = cols.start % tile
            keys = min(tile, first_query + ATTN_GROUP)
            s = read_cols(buf.s, cols, slice(0, keys))
            key = lax.broadcasted_iota(jnp.int32, s.shape, 0)
            query = lax.broadcasted_iota(jnp.int32, s.shape, 1) + first_query
            s = jnp.where(key <= query, s, MASK_VALUE)
            block_max = jnp.max(s, axis=0, keepdims=True)
        else:
            keys = tile
            s = read_cols(buf.s, cols)
            block_max = buf.smax[:, cols]
        m_prev = m_scr[:, cols]
        m_new = jnp.maximum(m_prev, block_max)
        p = jnp.exp2(s - m_new).astype(BF16)
        alpha = jnp.exp2(m_prev - m_new)
        m_scr[:, cols] = m_new
        write_cols(acc_scr, cols, alpha * read_cols(acc_scr, cols) + jnp.dot(
            vt_ref[0, 0, j, :, 0:keys], p, preferred_element_type=F32))

    def substep(j, cur, other):
        for cols in groups:
            scores(j + 1, other, cols)
            softmax(j, cur, cols, False)

    def last_block(i, cur):
        for cols in groups:
            softmax(i, cur, cols, True)
            scores(0, even, cols)

    lam = (jnp.exp(jnp.sum(lq1_ref[...] * lk1_ref[...], axis=-1, keepdims=True))
           - jnp.exp(jnp.sum(lq2_ref[...] * lk2_ref[...], axis=-1, keepdims=True))
           + lambda_init)

    def finish(i):
        acc1 = read_cols(acc_scr, slice(0, tile))
        acc2 = read_cols(acc_scr, slice(tile, 2 * tile))
        o1 = acc1[0:HEAD_W] / acc1[HEAD_W:HEAD_W + 1]
        o2 = acc2[0:HEAD_W] / acc2[HEAD_W:HEAD_W + 1]
        o = o1 - lam * o2
        o = o * lax.rsqrt(jnp.mean(o * o, axis=0, keepdims=True) + 1e-5)
        o = (o * g_ref[...]) * (1.0 - lambda_init)
        o_ref[0, pl.ds(pl.multiple_of(i * tile, tile), tile), :] = o.T.astype(BF16)

    def query_tile(i, carry):
        m_scr[...] = jnp.full(m_scr.shape, MASK_VALUE, F32)
        acc_scr[...] = jnp.zeros(acc_scr.shape, F32)
        i_next = jnp.minimum(i + 1, n_tiles - 1)

        def pair(j0):
            substep(j0, even, odd)
            substep(j0 + 1, odd, even)

        def quad(jj, c):
            pair(4 * jj)
            pair(4 * jj + 2)
            return c

        lax.fori_loop(0, lax.shift_right_logical(i, 2), quad, 0)

        @pl.when((i & 2) != 0)
        def _():
            pair(i & ~3)

        @pl.when((i & 1) == 1)
        def _():
            substep(i - 1, even, odd)
            load_queries(i_next)
            last_block(i, odd)

        @pl.when((i & 1) == 0)
        def _():
            load_queries(i_next)
            last_block(i, even)

        finish(i)
        return carry

    load_queries(0)
    for cols in groups:
        scores(0, even, cols)
    lax.fori_loop(0, n_tiles, query_tile, 0)


def _attn_call(qt, k, vt, lq1, lk1, lq2, lk2, subln_g_col, lambda_init):
    b, nh, s, _ = k.shape
    tile = qt.shape[-1]
    vec = lambda n: pl.BlockSpec((1, n), lambda bi, hi: (0, 0))
    return pl.pallas_call(
        functools.partial(_attn_kernel, tile=tile, lambda_init=lambda_init),
        grid=(b, nh),
        in_specs=[
            pl.BlockSpec((1, 1, s // tile, HEAD_W, tile), lambda bi, hi: (bi, hi, 0, 0, 0)),
            pl.BlockSpec((1, 1, s, HEAD_W), lambda bi, hi: (bi, hi, 0, 0)),
            pl.BlockSpec((1, 1, s // tile, V_ROWS, tile), lambda bi, hi: (bi, hi, 0, 0, 0)),
            vec(HEAD_DIM), vec(HEAD_DIM), vec(HEAD_DIM), vec(HEAD_DIM),
            pl.BlockSpec((HEAD_W, 1), lambda bi, hi: (0, 0)),
        ],
        out_specs=pl.BlockSpec((1, s, HEAD_W), lambda bi, hi: (bi, 0, hi)),
        out_shape=jax.ShapeDtypeStruct((b, s, nh * HEAD_W), BF16),
        scratch_shapes=[
            pltpu.VMEM((HEAD_W, 2 * tile), BF16),
            pltpu.VMEM((1, 2 * tile), F32),
            pltpu.VMEM((2 * tile // LANES, V_ROWS, LANES), F32),
            pltpu.VMEM((2 * tile // LANES, tile, LANES), F32),
            pltpu.VMEM((2 * tile // LANES, tile, LANES), F32),
            pltpu.VMEM((1, 2 * tile), F32), pltpu.VMEM((1, 2 * tile), F32),
        ],
        compiler_params=pltpu.CompilerParams(
            dimension_semantics=("arbitrary", "arbitrary"),
            vmem_limit_bytes=V7X_VMEM_LIMIT_BYTES),
        name="diff_attn",
    )(qt, k, vt, lq1, lk1, lq2, lk2, subln_g_col)


def _mixer_kernel(x_ref, mod_ref, h_ref, wg_ref, o_ref, u_ref, uh_ref, cw_ref, cb_ref, lg_ref, lb_ref,
                  wao_ref, wco_ref, wout_ref, x1_ref, ubuf, *, tm):
    i = pl.program_id(1)
    d = x_ref.shape[-1]
    x = x_ref[0]
    h = h_ref[0]
    halo = uh_ref[0]
    ubuf[0:CONV_HALO, :] = jnp.where(i > 0, halo, jnp.zeros_like(halo))
    ubuf[CONV_HALO:CONV_HALO + tm, :] = u_ref[0]
    ubuf[CONV_HALO + tm:CONV_HALO + tm + SUBLANES, :] = jnp.zeros((SUBLANES, CONV_W), F32)
    base = CONV_HALO - (CONV_K - 1)

    def conv_lanes(lanes):
        out = cb_ref[:, lanes]
        for r in range(SUBLANES):
            part = None
            for a in range((base + CONV_K - 1 - r) // SUBLANES + 1):
                j = SUBLANES * a + r - base
                if j < 0:
                    continue
                term = cw_ref[j:j + 1, lanes] * ubuf[SUBLANES * a:SUBLANES * a + tm + SUBLANES, lanes]
                part = term if part is None else part + term
            out = out + part[r:r + tm, :]
        return out

    n_chunks = CONV_W // LANES
    gate_cols = wg_ref.shape[1] // n_chunks
    gate_parts, conv_parts = [], []
    for c in range(n_chunks):
        z = jnp.dot(h, wg_ref[:, c * gate_cols:(c + 1) * gate_cols], preferred_element_type=F32)
        gate_parts.append(_sigmoid(z))
        conv_parts.append(conv_lanes(slice(c * LANES, (c + 1) * LANES)))
    gates = jnp.concatenate(gate_parts, axis=1)
    conv = jnp.concatenate(conv_parts, axis=1)

    mu = jnp.mean(conv, axis=-1, keepdims=True)
    cen = conv - mu
    var = jnp.mean(cen * cen, axis=-1, keepdims=True)
    y = (cen * lax.rsqrt(var + 1e-5)) * lg_ref[...] + lb_ref[...]
    act = (y * _sigmoid(y)).astype(BF16)

    y_conv = jnp.dot(act, wco_ref[...], preferred_element_type=F32)
    y_attn = jnp.dot(o_ref[0], wao_ref[...], preferred_element_type=F32)
    mix = (gates[:, 0:d] * y_attn + gates[:, d:2 * d] * y_conv).astype(BF16)
    mixed = jnp.dot(mix, wout_ref[...], preferred_element_type=F32)
    x1_ref[0] = x + (1.0 + mod_ref[0, 2:3, :]) * mixed


def _mixer_call(x, mod, h, wg, o, u, cw, cb, lg, lb, wao, wco, wout):
    b, s, d = x.shape
    tm = min(ROW_TILE, s)
    halo_blocks = tm // CONV_HALO
    row = lambda bi, i: (bi, i, 0)
    return pl.pallas_call(
        functools.partial(_mixer_kernel, tm=tm),
        grid=(b, s // tm),
        in_specs=[
            pl.BlockSpec((1, tm, d), row),
            pl.BlockSpec((1, N_MOD, d), lambda bi, i: (bi, 0, 0)),
            pl.BlockSpec((1, tm, d), row),
            _resident(wg.shape),
            pl.BlockSpec((1, tm, ATTN_W), row),
            pl.BlockSpec((1, tm, CONV_W), row),
            pl.BlockSpec((1, CONV_HALO, CONV_W), lambda bi, i: (bi, jnp.maximum(i * halo_blocks - 1, 0), 0)),
            _resident(cw.shape), _resident(cb.shape), _resident(lg.shape), _resident(lb.shape),
            _resident(wao.shape), _resident(wco.shape), _resident(wout.shape),
        ],
        out_specs=pl.BlockSpec((1, tm, d), row),
        out_shape=jax.ShapeDtypeStruct((b, s, d), F32),
        scratch_shapes=[pltpu.VMEM((CONV_HALO + tm + SUBLANES, CONV_W), F32)],
        compiler_params=pltpu.CompilerParams(
            dimension_semantics=("arbitrary", "arbitrary"),
            vmem_limit_bytes=V7X_VMEM_LIMIT_BYTES),
        name="mixer",
    )(x, mod, h, wg, o, u, u, cw, cb, lg, lb, wao, wco, wout)


def _ffn_kernel(x_ref, mod_ref, g_ref, wi_ref, wo_ref, fg_ref, out_ref, *, d_ff, final):
    x = x_ref[0]
    h = _modulated_rms_norm(x, g_ref[...], mod_ref[0, 4:5, :], mod_ref[0, 3:4, :]).astype(BF16)
    acc = jnp.zeros(x.shape, F32)
    for c0 in range(0, d_ff, FFN_CHUNK):
        cw = min(FFN_CHUNK, d_ff - c0)
        f_gate = jnp.dot(h, wi_ref[:, c0:c0 + cw], preferred_element_type=F32)
        f_up = jnp.dot(h, wi_ref[:, d_ff + c0:d_ff + c0 + cw], preferred_element_type=F32)
        a = ((f_gate * _sigmoid(f_gate)) * f_up).astype(BF16)
        acc = acc + jnp.dot(a, wo_ref[c0:c0 + cw, :], preferred_element_type=F32)
    x2 = x + (1.0 + mod_ref[0, 5:6, :]) * acc
    if final:
        x2 = (x2 * lax.rsqrt(jnp.mean(x2 * x2, axis=-1, keepdims=True) + 1e-6)) * fg_ref[...]
    out_ref[0] = x2


def _ffn_call(x, mod, g, wi, wo, final_g, final):
    b, s, d = x.shape
    d_ff = wo.shape[0]
    tm = min(FFN_ROW_TILE, s)
    row = lambda bi, i: (bi, i, 0)
    return pl.pallas_call(
        functools.partial(_ffn_kernel, d_ff=d_ff, final=final),
        grid=(b, s // tm),
        in_specs=[
            pl.BlockSpec((1, tm, d), row),
            pl.BlockSpec((1, N_MOD, d), lambda bi, i: (bi, 0, 0)),
            _resident((1, d)),
            _resident(wi.shape),
            _resident(wo.shape),
            _resident((1, d)),
        ],
        out_specs=pl.BlockSpec((1, tm, d), row),
        out_shape=jax.ShapeDtypeStruct((b, s, d), F32),
        compiler_params=pltpu.CompilerParams(
            dimension_semantics=("arbitrary", "arbitrary"),
            vmem_limit_bytes=V7X_VMEM_LIMIT_BYTES),
        name="ffn_final" if final else "ffn",
    )(x, mod, g, wi, wo, final_g)


def _rope_tables(positions):
    inv_freq = 1.0 / (ROPE_THETA ** (jnp.arange(0, ROT_DIM, 2, dtype=F32) / ROT_DIM))
    ang = positions.astype(F32)[..., None] * inv_freq
    return jnp.swapaxes(jnp.cos(ang), 1, 2), jnp.swapaxes(jnp.sin(ang), 1, 2)


def kernel(x, c, positions, ada_w, ada_b, norm1_g, w_in, lambda_q1, lambda_k1, lambda_q2, lambda_k2,
           subln_g, w_attn_o, dw_conv_w, dw_conv_b, conv_ln_g, conv_ln_b, w_conv_o, w_out, norm2_g,
           w_ffn_in, w_ffn_out, final_g):
    b, s, d = x.shape
    depth = ada_w.shape[0]
    assert all(s % min(t, s) == 0 for t in (ROW_TILE, FFN_ROW_TILE, ATTN_TILE))
    assert (2 * min(ATTN_TILE, s)) % ATTN_GROUP == 0
    assert ada_w.shape[2] == N_MOD * d and ada_w.shape[2] % ADA_COLS == 0
    assert w_in.shape[2] == N_QKVU + 2 * d

    rows = -(-b // 8) * 8
    c_pad = jnp.zeros((rows, d), F32).at[:b].set(c)
    mod_all = _ada_call(c_pad, ada_w, ada_b)[:, :b].reshape(depth, b, N_MOD, d)

    cos_t, sin_t = _rope_tables(positions)
    final_row = final_g.reshape(1, d)

    for l in range(depth):
        lambda_init = 0.8 - 0.6 * math.exp(-0.3 * l)
        mod = mod_all[l]
        g1 = norm1_g[l].reshape(1, d)
        w_l = w_in[l]
        qt, k, vt, u, h = _inproj_call(x, mod, g1, w_l[:, :3 * ATTN_W].T.astype(BF16),
                                    w_l[:, 3 * ATTN_W:N_QKVU].astype(BF16), cos_t, sin_t)
        o = _attn_call(qt, k, vt,
                       lambda_q1[l].reshape(1, HEAD_DIM), lambda_k1[l].reshape(1, HEAD_DIM),
                       lambda_q2[l].reshape(1, HEAD_DIM), lambda_k2[l].reshape(1, HEAD_DIM),
                       subln_g[l].reshape(HEAD_W, 1), lambda_init)
        x = _mixer_call(x, mod, h, w_l[:, N_QKVU:].astype(BF16), o, u,
                        dw_conv_w[l].reshape(CONV_K, CONV_W), dw_conv_b[l].reshape(1, CONV_W),
                        conv_ln_g[l].reshape(1, CONV_W), conv_ln_b[l].reshape(1, CONV_W),
                        w_attn_o[l].astype(BF16), w_conv_o[l].astype(BF16), w_out[l].astype(BF16))
        x = _ffn_call(x, mod, norm2_g[l].reshape(1, d), w_ffn_in[l].astype(BF16),
                      w_ffn_out[l].astype(BF16), final_row, final=(l == depth - 1))
    return x
```

```python
import collections
import functools
import math

import jax
import jax.numpy as jnp
from jax import lax
from jax.experimental import pallas as pl
from jax.experimental.pallas import tpu as pltpu

F32 = jnp.float32
BF16 = jnp.bfloat16

ATTN_HEADS = 4
HEAD_DIM = 64
HEAD_W = 2 * HEAD_DIM
ATTN_W = ATTN_HEADS * HEAD_W
ROT_DIM = HEAD_DIM // 4
ROPE_THETA = 500000.0
CONV_W = 512
CONV_K = 31
N_MOD = 6
N_QKVU = 3 * ATTN_W + 2 * CONV_W
CONV_HALO = 32
MASK_VALUE = -1e30
Q_SCALE = math.log2(math.e) / math.sqrt(HEAD_DIM)
V_ROWS = HEAD_W + 16

V7X_VMEM_LIMIT_BYTES = 56 * 1024 * 1024
V7X_MXU_COLS = 256
SUBLANES = 8
LANES = 128

ROW_TILE = 512
FFN_ROW_TILE = 1024
ATTN_TILE = 512
ATTN_GROUP = V7X_MXU_COLS
ADA_COLS = 1024
FFN_CHUNK = 1024


def _sigmoid(x):
    return 0.5 * jnp.tanh(0.5 * x) + 0.5


def _modulated_rms_norm(x, g, scale, shift, eps=1e-6):
    y = x * lax.rsqrt(jnp.mean(x * x, axis=-1, keepdims=True) + eps)
    return (y * g) * (1.0 + scale) + shift


def _resident(shape):
    zeros = (0,) * len(shape)
    return pl.BlockSpec(shape, lambda *_: zeros, pipeline_mode=pl.Buffered(1))


def _ada_kernel(c_ref, w_ref, b_ref, o_ref):
    c = c_ref[...]
    c_act = (c * _sigmoid(c)).astype(BF16)
    w = w_ref[0].astype(BF16)
    o_ref[0] = jnp.dot(c_act, w, preferred_element_type=F32) + b_ref[0]


def _ada_call(c_pad, ada_w, ada_b):
    depth, d, n = ada_w.shape
    rows = c_pad.shape[0]
    return pl.pallas_call(
        _ada_kernel,
        grid=(depth, n // ADA_COLS),
        in_specs=[
            pl.BlockSpec((rows, d), lambda l, j: (0, 0)),
            pl.BlockSpec((1, d, ADA_COLS), lambda l, j: (l, 0, j)),
            pl.BlockSpec((1, 1, ADA_COLS), lambda l, j: (l, 0, j)),
        ],
        out_specs=pl.BlockSpec((1, rows, ADA_COLS), lambda l, j: (l, 0, j)),
        out_shape=jax.ShapeDtypeStruct((depth, rows, n), F32),
        compiler_params=pltpu.CompilerParams(
            dimension_semantics=("arbitrary", "arbitrary"),
            vmem_limit_bytes=V7X_VMEM_LIMIT_BYTES),
        name="ada_proj",
    )(c_pad, ada_w, ada_b.reshape(depth, 1, n))


_NT_DIMS = (((1,), (1,)), ((), ()))

_StageBuffers = collections.namedtuple("_StageBuffers", ["s", "smax"])


def _rope_rows(z, cos, sin):
    half = ROT_DIM // 2
    parts = []
    for comp in range(2):
        r0 = comp * HEAD_DIM
        x1 = z[r0:r0 + half]
        x2 = z[r0 + half:r0 + ROT_DIM]
        parts += [x1 * cos - x2 * sin, x2 * cos + x1 * sin, z[r0 + ROT_DIM:r0 + HEAD_DIM]]
    return jnp.concatenate(parts, axis=0)


def _inproj_kernel(x_ref, mod_ref, g_ref, wt_ref, wc_ref, cos_ref, sin_ref,
                   qt_ref, k_ref, vt_ref, u_ref, h_ref):
    x = x_ref[0]
    h = _modulated_rms_norm(x, g_ref[...], mod_ref[0, 1:2, :], mod_ref[0, 0:1, :]).astype(BF16)
    h_ref[0] = h
    cos = cos_ref[0]
    sin = sin_ref[0]

    zq = lax.dot_general(wt_ref[0:ATTN_W, :], h, _NT_DIMS, preferred_element_type=F32)
    for hd in range(ATTN_HEADS):
        t = _rope_rows(zq[hd * HEAD_W:(hd + 1) * HEAD_W], cos, sin)
        qt_ref[0, hd, 0] = (t * Q_SCALE).astype(BF16)

    zk = lax.dot_general(wt_ref[ATTN_W:2 * ATTN_W, :], h, _NT_DIMS, preferred_element_type=F32)
    for hd in range(ATTN_HEADS):
        k_ref[0, hd] = _rope_rows(zk[hd * HEAD_W:(hd + 1) * HEAD_W], cos, sin).T.astype(BF16)

    zv = lax.dot_general(wt_ref[2 * ATTN_W:3 * ATTN_W, :], h, _NT_DIMS, preferred_element_type=F32)
    for hd in range(ATTN_HEADS):
        vt_ref[0, hd, 0, 0:HEAD_W, :] = zv[hd * HEAD_W:(hd + 1) * HEAD_W].astype(BF16)
        vt_ref[0, hd, 0, HEAD_W:V_ROWS, :] = jnp.ones((V_ROWS - HEAD_W, zv.shape[1]), BF16)

    ga = jnp.dot(h, wc_ref[:, 0:CONV_W], preferred_element_type=F32)
    gb = jnp.dot(h, wc_ref[:, CONV_W:2 * CONV_W], preferred_element_type=F32)
    u_ref[0] = ga * _sigmoid(gb)


def _inproj_call(x, mod, g, wt, wc, cos_t, sin_t):
    b, s, d = x.shape
    tm = min(ATTN_TILE, s)
    row = lambda bi, i: (bi, i, 0)
    t_spec = lambda rows: pl.BlockSpec((1, ATTN_HEADS, 1, rows, tm), lambda bi, i: (bi, 0, i, 0, 0))
    t_shape = lambda rows: jax.ShapeDtypeStruct((b, ATTN_HEADS, s // tm, rows, tm), BF16)
    rot_spec = pl.BlockSpec((1, ROT_DIM // 2, tm), lambda bi, i: (bi, 0, i))
    return pl.pallas_call(
        _inproj_kernel,
        grid=(b, s // tm),
        in_specs=[
            pl.BlockSpec((1, tm, d), row),
            pl.BlockSpec((1, N_MOD, d), lambda bi, i: (bi, 0, 0)),
            _resident((1, d)),
            _resident(wt.shape),
            _resident(wc.shape),
            rot_spec, rot_spec,
        ],
        out_specs=[
            t_spec(HEAD_W),
            pl.BlockSpec((1, ATTN_HEADS, tm, HEAD_W), lambda bi, i: (bi, 0, i, 0)),
            t_spec(V_ROWS),
            pl.BlockSpec((1, tm, CONV_W), row),
            pl.BlockSpec((1, tm, d), row),
        ],
        out_shape=[
            t_shape(HEAD_W),
            jax.ShapeDtypeStruct((b, ATTN_HEADS, s, HEAD_W), BF16),
            t_shape(V_ROWS),
            jax.ShapeDtypeStruct((b, s, CONV_W), F32),
            jax.ShapeDtypeStruct((b, s, d), BF16),
        ],
        compiler_params=pltpu.CompilerParams(
            dimension_semantics=("arbitrary", "arbitrary"),
            vmem_limit_bytes=V7X_VMEM_LIMIT_BYTES),
        name="in_proj",
    )(x, mod, g, wt, wc, cos_t, sin_t)


def _attn_kernel(qt_ref, k_ref, vt_ref, lq1_ref, lk1_ref, lq2_ref, lk2_ref, g_ref, o_ref,
                 qbd_scr, m_scr, acc_scr, s0, s1, x0, x1, *, tile, lambda_init):
    n_tiles = qt_ref.shape[2]
    groups = [slice(c0, c0 + ATTN_GROUP) for c0 in range(0, 2 * tile, ATTN_GROUP)]
    even = _StageBuffers(s0, x0)
    odd = _StageBuffers(s1, x1)

    def load_queries(i):
        qt = qt_ref[0, 0, i]
        first = lax.broadcasted_iota(jnp.int32, qt.shape, 0) < HEAD_DIM
        zero = jnp.zeros_like(qt)
        qbd_scr[:, 0:tile] = jnp.where(first, qt, zero)
        qbd_scr[:, tile:2 * tile] = jnp.where(first, zero, qt)

    def read_cols(ref, cols, rows=slice(None)):
        return jnp.concatenate([ref[t, rows, :] for t in range(cols.start // LANES, cols.stop // LANES)], axis=1)

    def write_cols(ref, cols, value):
        for n, t in enumerate(range(cols.start // LANES, cols.stop // LANES)):
            ref[t] = value[:, n * LANES:(n + 1) * LANES]

    def scores(j, buf, cols):
        start = pl.multiple_of(j * tile, tile)
        kb = k_ref[0, 0, pl.ds(start, tile), :]
        s = jnp.dot(kb, qbd_scr[:, cols], preferred_element_type=F32)
        write_cols(buf.s, cols, s)
        buf.smax[:, cols] = jnp.max(s, axis=0, keepdims=True)

    def softmax(j, buf, cols, masked):
        if masked:
            first_query = cols.start % tile
            keys = min(tile, first_query + ATTN_GROUP)
            s = read_cols(buf.s, cols, slice(0, keys))
            key = lax.broadcasted_iota(jnp.int32, s.shape, 0)
            query = lax.broadcasted_iota(jnp.int32, s.shape, 1) + first_query
            s = jnp.where(key <= query, s, MASK_VALUE)
            block_max = jnp.max(s, axis=0, keepdims=True)
        else:
            keys = tile
            s = read_cols(buf.s, cols)
            block_max = buf.smax[:, cols]
        m_prev = m_scr[:, cols]
        m_new = jnp.maximum(m_prev, block_max)
        p = jnp.exp2(s - m_new).astype(BF16)
        alpha = jnp.exp2(m_prev - m_new)
        m_scr[:, cols] = m_new
        write_cols(acc_scr, cols, alpha * read_cols(acc_scr, cols) + jnp.dot(
            vt_ref[0, 0, j, :, 0:keys], p, preferred_element_type=F32))

    def substep(j, cur, other):
        for cols in groups:
            scores(j + 1, other, cols)
            softmax(j, cur, cols, False)

    def last_block(i, cur):
        for cols in groups:
            softmax(i, cur, cols, True)
            scores(0, even, cols)

    lam = (jnp.exp(jnp.sum(lq1_ref[...] * lk1_ref[...], axis=-1, keepdims=True))
           - jnp.exp(jnp.sum(lq2_ref[...] * lk2_ref[...], axis=-1, keepdims=True))
           + lambda_init)

    def finish(i):
        acc1 = read_cols(acc_scr, slice(0, tile))
        acc2 = read_cols(acc_scr, slice(tile, 2 * tile))
        o1 = acc1[0:HEAD_W] / acc1[HEAD_W:HEAD_W + 1]
        o2 = acc2[0:HEAD_W] / acc2[HEAD_W:HEAD_W + 1]
        o = o1 - lam * o2
        o = o * lax.rsqrt(jnp.mean(o * o, axis=0, keepdims=True) + 1e-5)
        o = (o * g_ref[...]) * (1.0 - lambda_init)
        o_ref[0, pl.ds(pl.multiple_of(i * tile, tile), tile), :] = o.T.astype(BF16)

    def query_tile(i, carry):
        m_scr[...] = jnp.full(m_scr.shape, MASK_VALUE, F32)
        acc_scr[...] = jnp.zeros(acc_scr.shape, F32)
        i_next = jnp.minimum(i + 1, n_tiles - 1)

        parity = (even, odd)

        def quad(jj, c):
            for n in range(4):
                substep(4 * jj + n, parity[n % 2], parity[(n + 1) % 2])
            return c

        lax.fori_loop(0, lax.shift_right_logical(i, 2), quad, 0)

        def tail(leftover):
            for n in range(leftover):
                substep(i - leftover + n, parity[n % 2], parity[(n + 1) % 2])
            load_queries(i_next)
            last_block(i, parity[leftover % 2])
            finish(i)

        for leftover in range(4):
            pl.when((i & 3) == leftover)(functools.partial(tail, leftover))
        return carry

    load_queries(0)
    for cols in groups:
        scores(0, even, cols)
    lax.fori_loop(0, n_tiles, query_tile, 0)


def _attn_call(qt, k, vt, lq1, lk1, lq2, lk2, subln_g_col, lambda_init):
    b, nh, s, _ = k.shape
    tile = qt.shape[-1]
    vec = lambda n: pl.BlockSpec((1, n), lambda bi, hi: (0, 0))
    return pl.pallas_call(
        functools.partial(_attn_kernel, tile=tile, lambda_init=lambda_init),
        grid=(b, nh),
        in_specs=[
            pl.BlockSpec((1, 1, s // tile, HEAD_W, tile), lambda bi, hi: (bi, hi, 0, 0, 0)),
            pl.BlockSpec((1, 1, s, HEAD_W), lambda bi, hi: (bi, hi, 0, 0)),
            pl.BlockSpec((1, 1, s // tile, V_ROWS, tile), lambda bi, hi: (bi, hi, 0, 0, 0)),
            vec(HEAD_DIM), vec(HEAD_DIM), vec(HEAD_DIM), vec(HEAD_DIM),
            pl.BlockSpec((HEAD_W, 1), lambda bi, hi: (0, 0)),
        ],
        out_specs=pl.BlockSpec((1, s, HEAD_W), lambda bi, hi: (bi, 0, hi)),
        out_shape=jax.ShapeDtypeStruct((b, s, nh * HEAD_W), BF16),
        scratch_shapes=[
            pltpu.VMEM((HEAD_W, 2 * tile), BF16),
            pltpu.VMEM((1, 2 * tile), F32),
            pltpu.VMEM((2 * tile // LANES, V_ROWS, LANES), F32),
            pltpu.VMEM((2 * tile // LANES, tile, LANES), F32),
            pltpu.VMEM((2 * tile // LANES, tile, LANES), F32),
            pltpu.VMEM((1, 2 * tile), F32), pltpu.VMEM((1, 2 * tile), F32),
        ],
        compiler_params=pltpu.CompilerParams(
            dimension_semantics=("arbitrary", "arbitrary"),
            vmem_limit_bytes=V7X_VMEM_LIMIT_BYTES),
        name="diff_attn",
    )(qt, k, vt, lq1, lk1, lq2, lk2, subln_g_col)


def _mixer_kernel(x_ref, mod_ref, h_ref, wg_ref, o_ref, u_ref, uh_ref, cw_ref, cb_ref, lg_ref, lb_ref,
                  wao_ref, wco_ref, wout_ref, x1_ref, ubuf, *, tm):
    i = pl.program_id(1)
    d = x_ref.shape[-1]
    x = x_ref[0]
    h = h_ref[0]
    halo = uh_ref[0]
    ubuf[0:CONV_HALO, :] = jnp.where(i > 0, halo, jnp.zeros_like(halo))
    ubuf[CONV_HALO:CONV_HALO + tm, :] = u_ref[0]
    ubuf[CONV_HALO + tm:CONV_HALO + tm + SUBLANES, :] = jnp.zeros((SUBLANES, CONV_W), F32)
    base = CONV_HALO - (CONV_K - 1)

    def conv_lanes(lanes):
        out = cb_ref[:, lanes]
        for r in range(SUBLANES):
            part = None
            for a in range((base + CONV_K - 1 - r) // SUBLANES + 1):
                j = SUBLANES * a + r - base
                if j < 0:
                    continue
                term = cw_ref[j:j + 1, lanes] * ubuf[SUBLANES * a:SUBLANES * a + tm + SUBLANES, lanes]
                part = term if part is None else part + term
            out = out + part[r:r + tm, :]
        return out

    n_chunks = CONV_W // LANES
    gate_cols = wg_ref.shape[1] // n_chunks
    gate_parts, conv_parts = [], []
    for c in range(n_chunks):
        z = jnp.dot(h, wg_ref[:, c * gate_cols:(c + 1) * gate_cols], preferred_element_type=F32)
        gate_parts.append(_sigmoid(z))
        conv_parts.append(conv_lanes(slice(c * LANES, (c + 1) * LANES)))
    gates = jnp.concatenate(gate_parts, axis=1)
    conv = jnp.concatenate(conv_parts, axis=1)

    mu = jnp.mean(conv, axis=-1, keepdims=True)
    cen = conv - mu
    var = jnp.mean(cen * cen, axis=-1, keepdims=True)
    y = (cen * lax.rsqrt(var + 1e-5)) * lg_ref[...] + lb_ref[...]
    act = (y * _sigmoid(y)).astype(BF16)

    y_conv = jnp.dot(act, wco_ref[...], preferred_element_type=F32)
    y_attn = jnp.dot(o_ref[0], wao_ref[...], preferred_element_type=F32)
    mix = (gates[:, 0:d] * y_attn + gates[:, d:2 * d] * y_conv).astype(BF16)
    mixed = jnp.dot(mix, wout_ref[...], preferred_element_type=F32)
    x1_ref[0] = x + (1.0 + mod_ref[0, 2:3, :]) * mixed


def _mixer_call(x, mod, h, wg, o, u, cw, cb, lg, lb, wao, wco, wout):
    b, s, d = x.shape
    tm = min(ROW_TILE, s)
    halo_blocks = tm // CONV_HALO
    row = lambda bi, i: (bi, i, 0)
    return pl.pallas_call(
        functools.partial(_mixer_kernel, tm=tm),
        grid=(b, s // tm),
        in_specs=[
            pl.BlockSpec((1, tm, d), row),
            pl.BlockSpec((1, N_MOD, d), lambda bi, i: (bi, 0, 0)),
            pl.BlockSpec((1, tm, d), row),
            _resident(wg.shape),
            pl.BlockSpec((1, tm, ATTN_W), row),
            pl.BlockSpec((1, tm, CONV_W), row),
            pl.BlockSpec((1, CONV_HALO, CONV_W), lambda bi, i: (bi, jnp.maximum(i * halo_blocks - 1, 0), 0)),
            _resident(cw.shape), _resident(cb.shape), _resident(lg.shape), _resident(lb.shape),
            _resident(wao.shape), _resident(wco.shape), _resident(wout.shape),
        ],
        out_specs=pl.BlockSpec((1, tm, d), row),
        out_shape=jax.ShapeDtypeStruct((b, s, d), F32),
        scratch_shapes=[pltpu.VMEM((CONV_HALO + tm + SUBLANES, CONV_W), F32)],
        compiler_params=pltpu.CompilerParams(
            dimension_semantics=("arbitrary", "arbitrary"),
            vmem_limit_bytes=V7X_VMEM_LIMIT_BYTES),
        name="mixer",
    )(x, mod, h, wg, o, u, u, cw, cb, lg, lb, wao, wco, wout)


def _ffn_kernel(x_ref, mod_ref, g_ref, wi_ref, wo_ref, fg_ref, out_ref, *, d_ff, final):
    x = x_ref[0]
    h = _modulated_rms_norm(x, g_ref[...], mod_ref[0, 4:5, :], mod_ref[0, 3:4, :]).astype(BF16)
    acc = jnp.zeros(x.shape, F32)
    for c0 in range(0, d_ff, FFN_CHUNK):
        cw = min(FFN_CHUNK, d_ff - c0)
        f_gate = jnp.dot(h, wi_ref[:, c0:c0 + cw], preferred_element_type=F32)
        f_up = jnp.dot(h, wi_ref[:, d_ff + c0:d_ff + c0 + cw], preferred_element_type=F32)
        a = ((f_gate * _sigmoid(f_gate)) * f_up).astype(BF16)
        acc = acc + jnp.dot(a, wo_ref[c0:c0 + cw, :], preferred_element_type=F32)
    x2 = x + (1.0 + mod_ref[0, 5:6, :]) * acc
    if final:
        x2 = (x2 * lax.rsqrt(jnp.mean(x2 * x2, axis=-1, keepdims=True) + 1e-6)) * fg_ref[...]
    out_ref[0] = x2


def _ffn_call(x, mod, g, wi, wo, final_g, final):
    b, s, d = x.shape
    d_ff = wo.shape[0]
    tm = min(FFN_ROW_TILE, s)
    row = lambda bi, i: (bi, i, 0)
    return pl.pallas_call(
        functools.partial(_ffn_kernel, d_ff=d_ff, final=final),
        grid=(b, s // tm),
        in_specs=[
            pl.BlockSpec((1, tm, d), row),
            pl.BlockSpec((1, N_MOD, d), lambda bi, i: (bi, 0, 0)),
            _resident((1, d)),
            _resident(wi.shape),
            _resident(wo.shape),
            _resident((1, d)),
        ],
        out_specs=pl.BlockSpec((1, tm, d), row),
        out_shape=jax.ShapeDtypeStruct((b, s, d), F32),
        compiler_params=pltpu.CompilerParams(
            dimension_semantics=("arbitrary", "arbitrary"),
            vmem_limit_bytes=V7X_VMEM_LIMIT_BYTES),
        name="ffn_final" if final else "ffn",
    )(x, mod, g, wi, wo, final_g)


def _rope_tables(positions):
    inv_freq = 1.0 / (ROPE_THETA ** (jnp.arange(0, ROT_DIM, 2, dtype=F32) / ROT_DIM))
    ang = positions.astype(F32)[..., None] * inv_freq
    return jnp.swapaxes(jnp.cos(ang), 1, 2), jnp.swapaxes(jnp.sin(ang), 1, 2)


def kernel(x, c, positions, ada_w, ada_b, norm1_g, w_in, lambda_q1, lambda_k1, lambda_q2, lambda_k2,
           subln_g, w_attn_o, dw_conv_w, dw_conv_b, conv_ln_g, conv_ln_b, w_conv_o, w_out, norm2_g,
           w_ffn_in, w_ffn_out, final_g):
    b, s, d = x.shape
    depth = ada_w.shape[0]
    assert all(s % min(t, s) == 0 for t in (ROW_TILE, FFN_ROW_TILE, ATTN_TILE))
    assert (2 * min(ATTN_TILE, s)) % ATTN_GROUP == 0
    assert ada_w.shape[2] == N_MOD * d and ada_w.shape[2] % ADA_COLS == 0
    assert w_in.shape[2] == N_QKVU + 2 * d

    rows = -(-b // 8) * 8
    c_pad = jnp.zeros((rows, d), F32).at[:b].set(c)
    mod_all = _ada_call(c_pad, ada_w, ada_b)[:, :b].reshape(depth, b, N_MOD, d)

    cos_t, sin_t = _rope_tables(positions)
    final_row = final_g.reshape(1, d)

    for l in range(depth):
        lambda_init = 0.8 - 0.6 * math.exp(-0.3 * l)
        mod = mod_all[l]
        g1 = norm1_g[l].reshape(1, d)
        w_l = w_in[l]
        qt, k, vt, u, h = _inproj_call(x, mod, g1, w_l[:, :3 * ATTN_W].T.astype(BF16),
                                    w_l[:, 3 * ATTN_W:N_QKVU].astype(BF16), cos_t, sin_t)
        o = _attn_call(qt, k, vt,
                       lambda_q1[l].reshape(1, HEAD_DIM), lambda_k1[l].reshape(1, HEAD_DIM),
                       lambda_q2[l].reshape(1, HEAD_DIM), lambda_k2[l].reshape(1, HEAD_DIM),
                       subln_g[l].reshape(HEAD_W, 1), lambda_init)
        x = _mixer_call(x, mod, h, w_l[:, N_QKVU:].astype(BF16), o, u,
                        dw_conv_w[l].reshape(CONV_K, CONV_W), dw_conv_b[l].reshape(1, CONV_W),
                        conv_ln_g[l].reshape(1, CONV_W), conv_ln_b[l].reshape(1, CONV_W),
                        w_attn_o[l].astype(BF16), w_conv_o[l].astype(BF16), w_out[l].astype(BF16))
        x = _ffn_call(x, mod, norm2_g[l].reshape(1, d), w_ffn_in[l].astype(BF16),
                      w_ffn_out[l].astype(BF16), final_row, final=(l == depth - 1))
    return x
```

```python
import collections
import functools
import math

import jax
import jax.numpy as jnp
from jax import lax
from jax.experimental import pallas as pl
from jax.experimental.pallas import tpu as pltpu

F32 = jnp.float32
BF16 = jnp.bfloat16

ATTN_HEADS = 4
HEAD_DIM = 64
HEAD_W = 2 * HEAD_DIM
ATTN_W = ATTN_HEADS * HEAD_W
ROT_DIM = HEAD_DIM // 4
ROPE_THETA = 500000.0
CONV_W = 512
CONV_K = 31
N_MOD = 6
N_QKVU = 3 * ATTN_W + 2 * CONV_W
CONV_HALO = 32
MASK_VALUE = -1e30
Q_SCALE = math.log2(math.e) / math.sqrt(HEAD_DIM)

V7X_VMEM_LIMIT_BYTES = 56 * 1024 * 1024
V7X_MXU_COLS = 256
SUBLANES = 8
BF16_SUBLANES = 2 * SUBLANES
LANES = 128
V_ROWS = HEAD_W + BF16_SUBLANES

ROW_TILE = 512
FFN_ROW_TILE = 1024
ATTN_TILE = 512
ATTN_GROUP = V7X_MXU_COLS
ADA_COLS = 1024
FFN_CHUNK = 1024


def _sigmoid(x):
    return 0.5 * jnp.tanh(0.5 * x) + 0.5


def _modulated_rms_norm(x, g, scale, shift, eps=1e-6):
    y = x * lax.rsqrt(jnp.mean(x * x, axis=-1, keepdims=True) + eps)
    return (y * g) * (1.0 + scale) + shift


def _resident(shape):
    zeros = (0,) * len(shape)
    return pl.BlockSpec(shape, lambda *_: zeros, pipeline_mode=pl.Buffered(1))


def _ada_kernel(c_ref, w_ref, b_ref, o_ref):
    c = c_ref[...]
    c_act = (c * _sigmoid(c)).astype(BF16)
    w = w_ref[0].astype(BF16)
    o_ref[0] = jnp.dot(c_act, w, preferred_element_type=F32) + b_ref[0]


def _ada_call(c_pad, ada_w, ada_b):
    depth, d, n = ada_w.shape
    rows = c_pad.shape[0]
    return pl.pallas_call(
        _ada_kernel,
        grid=(depth, n // ADA_COLS),
        in_specs=[
            pl.BlockSpec((rows, d), lambda l, j: (0, 0)),
            pl.BlockSpec((1, d, ADA_COLS), lambda l, j: (l, 0, j)),
            pl.BlockSpec((1, 1, ADA_COLS), lambda l, j: (l, 0, j)),
        ],
        out_specs=pl.BlockSpec((1, rows, ADA_COLS), lambda l, j: (l, 0, j)),
        out_shape=jax.ShapeDtypeStruct((depth, rows, n), F32),
        compiler_params=pltpu.CompilerParams(
            dimension_semantics=("arbitrary", "arbitrary"),
            vmem_limit_bytes=V7X_VMEM_LIMIT_BYTES),
        name="ada_proj",
    )(c_pad, ada_w, ada_b.reshape(depth, 1, n))


_NT_DIMS = (((1,), (1,)), ((), ()))


def _rope_rows(z, cos, sin):
    half = ROT_DIM // 2
    parts = []
    for comp in range(2):
        r0 = comp * HEAD_DIM
        x1 = z[r0:r0 + half]
        x2 = z[r0 + half:r0 + ROT_DIM]
        parts += [x1 * cos - x2 * sin, x2 * cos + x1 * sin, z[r0 + ROT_DIM:r0 + HEAD_DIM]]
    return jnp.concatenate(parts, axis=0)


def _inproj_kernel(x_ref, mod_ref, g_ref, wt_ref, wc_ref, cos_ref, sin_ref,
                   qt_ref, k_ref, vt_ref, u_ref, h_ref):
    x = x_ref[0]
    h = _modulated_rms_norm(x, g_ref[...], mod_ref[0, 1:2, :], mod_ref[0, 0:1, :]).astype(BF16)
    h_ref[0] = h
    cos = cos_ref[0]
    sin = sin_ref[0]

    zq = lax.dot_general(wt_ref[0:ATTN_W, :], h, _NT_DIMS, preferred_element_type=F32)
    for hd in range(ATTN_HEADS):
        t = _rope_rows(zq[hd * HEAD_W:(hd + 1) * HEAD_W], cos, sin)
        qt_ref[0, hd, 0] = (t * Q_SCALE).astype(BF16)

    zk = lax.dot_general(wt_ref[ATTN_W:2 * ATTN_W, :], h, _NT_DIMS, preferred_element_type=F32)
    for hd in range(ATTN_HEADS):
        k_ref[0, hd] = _rope_rows(zk[hd * HEAD_W:(hd + 1) * HEAD_W], cos, sin).T.astype(BF16)

    zv = lax.dot_general(wt_ref[2 * ATTN_W:3 * ATTN_W, :], h, _NT_DIMS, preferred_element_type=F32)
    for hd in range(ATTN_HEADS):
        vt_ref[0, hd, 0, 0:HEAD_W, :] = zv[hd * HEAD_W:(hd + 1) * HEAD_W].astype(BF16)
        vt_ref[0, hd, 0, HEAD_W:V_ROWS, :] = jnp.ones((V_ROWS - HEAD_W, zv.shape[1]), BF16)

    ga = jnp.dot(h, wc_ref[:, 0:CONV_W], preferred_element_type=F32)
    gb = jnp.dot(h, wc_ref[:, CONV_W:2 * CONV_W], preferred_element_type=F32)
    u_ref[0] = ga * _sigmoid(gb)


def _inproj_call(x, mod, g, wt, wc, cos_t, sin_t):
    b, s, d = x.shape
    tm = min(ATTN_TILE, s)
    row = lambda bi, i: (bi, i, 0)
    t_spec = lambda rows: pl.BlockSpec((1, ATTN_HEADS, 1, rows, tm), lambda bi, i: (bi, 0, i, 0, 0))
    t_shape = lambda rows: jax.ShapeDtypeStruct((b, ATTN_HEADS, s // tm, rows, tm), BF16)
    rot_spec = pl.BlockSpec((1, ROT_DIM // 2, tm), lambda bi, i: (bi, 0, i))
    return pl.pallas_call(
        _inproj_kernel,
        grid=(b, s // tm),
        in_specs=[
            pl.BlockSpec((1, tm, d), row),
            pl.BlockSpec((1, N_MOD, d), lambda bi, i: (bi, 0, 0)),
            _resident((1, d)),
            _resident(wt.shape),
            _resident(wc.shape),
            rot_spec, rot_spec,
        ],
        out_specs=[
            t_spec(HEAD_W),
            pl.BlockSpec((1, ATTN_HEADS, tm, HEAD_W), lambda bi, i: (bi, 0, i, 0)),
            t_spec(V_ROWS),
            pl.BlockSpec((1, tm, CONV_W), row),
            pl.BlockSpec((1, tm, d), row),
        ],
        out_shape=[
            t_shape(HEAD_W),
            jax.ShapeDtypeStruct((b, ATTN_HEADS, s, HEAD_W), BF16),
            t_shape(V_ROWS),
            jax.ShapeDtypeStruct((b, s, CONV_W), F32),
            jax.ShapeDtypeStruct((b, s, d), BF16),
        ],
        compiler_params=pltpu.CompilerParams(
            dimension_semantics=("arbitrary", "arbitrary"),
            vmem_limit_bytes=V7X_VMEM_LIMIT_BYTES),
        name="in_proj",
    )(x, mod, g, wt, wc, cos_t, sin_t)


_StageBuffers = collections.namedtuple("_StageBuffers", ["s", "smax"])


def _attn_kernel(qt_ref, k_ref, vt_ref, lq1_ref, lk1_ref, lq2_ref, lk2_ref, g_ref, o_ref,
                 qbd_scr, m_scr, acc_scr, s0, s1, x0, x1, *, tile, lambda_init):
    n_tiles = qt_ref.shape[2]
    groups = [slice(c0, c0 + ATTN_GROUP) for c0 in range(0, 2 * tile, ATTN_GROUP)]
    even = _StageBuffers(s0, x0)
    odd = _StageBuffers(s1, x1)

    def load_queries(i):
        qt = qt_ref[0, 0, i]
        first = lax.broadcasted_iota(jnp.int32, qt.shape, 0) < HEAD_DIM
        zero = jnp.zeros_like(qt)
        qbd_scr[:, 0:tile] = jnp.where(first, qt, zero)
        qbd_scr[:, tile:2 * tile] = jnp.where(first, zero, qt)

    def read_cols(ref, cols, rows=slice(None)):
        return jnp.concatenate([ref[t, rows, :] for t in range(cols.start // LANES, cols.stop // LANES)], axis=1)

    def write_cols(ref, cols, value):
        for n, t in enumerate(range(cols.start // LANES, cols.stop // LANES)):
            ref[t] = value[:, n * LANES:(n + 1) * LANES]

    def scores(j, buf, cols):
        start = pl.multiple_of(j * tile, tile)
        kb = k_ref[0, 0, pl.ds(start, tile), :]
        s = jnp.dot(kb, qbd_scr[:, cols], preferred_element_type=F32)
        write_cols(buf.s, cols, s)
        buf.smax[:, cols] = jnp.max(s, axis=0, keepdims=True)

    def softmax(j, buf, cols, masked):
        if masked:
            first_query = cols.start % tile
            keys = min(tile, first_query + ATTN_GROUP)
            s = read_cols(buf.s, cols, slice(0, keys))
            key = lax.broadcasted_iota(jnp.int32, s.shape, 0)
            query = lax.broadcasted_iota(jnp.int32, s.shape, 1) + first_query
            s = jnp.where(key <= query, s, MASK_VALUE)
            block_max = jnp.max(s, axis=0, keepdims=True)
        else:
            keys = tile
            s = read_cols(buf.s, cols)
            block_max = buf.smax[:, cols]
        m_prev = m_scr[:, cols]
        m_new = jnp.maximum(m_prev, block_max)
        p = jnp.exp2(s - m_new).astype(BF16)
        alpha = jnp.exp2(m_prev - m_new)
        m_scr[:, cols] = m_new
        write_cols(acc_scr, cols, alpha * read_cols(acc_scr, cols) + jnp.dot(
            vt_ref[0, 0, j, :, 0:keys], p, preferred_element_type=F32))

    def substep(j, cur, other):
        for cols in groups:
            scores(j + 1, other, cols)
            softmax(j, cur, cols, False)

    def last_block(i, cur):
        for cols in groups:
            softmax(i, cur, cols, True)
            scores(0, even, cols)

    lam = (jnp.exp(jnp.sum(lq1_ref[...] * lk1_ref[...], axis=-1, keepdims=True))
           - jnp.exp(jnp.sum(lq2_ref[...] * lk2_ref[...], axis=-1, keepdims=True))
           + lambda_init)

    def finish(i):
        acc1 = read_cols(acc_scr, slice(0, tile))
        acc2 = read_cols(acc_scr, slice(tile, 2 * tile))
        o1 = acc1[0:HEAD_W] / acc1[HEAD_W:HEAD_W + 1]
        o2 = acc2[0:HEAD_W] / acc2[HEAD_W:HEAD_W + 1]
        o = o1 - lam * o2
        o = o * lax.rsqrt(jnp.mean(o * o, axis=0, keepdims=True) + 1e-5)
        o = (o * g_ref[...]) * (1.0 - lambda_init)
        o_ref[0, pl.ds(pl.multiple_of(i * tile, tile), tile), :] = o.T.astype(BF16)

    def query_tile(i, carry):
        m_scr[...] = jnp.full(m_scr.shape, MASK_VALUE, F32)
        acc_scr[...] = jnp.zeros(acc_scr.shape, F32)
        i_next = jnp.minimum(i + 1, n_tiles - 1)

        parity = (even, odd)

        def quad(jj, c):
            for n in range(4):
                substep(4 * jj + n, parity[n % 2], parity[(n + 1) % 2])
            return c

        lax.fori_loop(0, lax.shift_right_logical(i, 2), quad, 0)

        def tail(leftover):
            for n in range(leftover):
                substep(i - leftover + n, parity[n % 2], parity[(n + 1) % 2])
            load_queries(i_next)
            last_block(i, parity[leftover % 2])
            finish(i)

        for leftover in range(4):
            pl.when((i & 3) == leftover)(functools.partial(tail, leftover))
        return carry

    load_queries(0)
    for cols in groups:
        scores(0, even, cols)
    lax.fori_loop(0, n_tiles, query_tile, 0)


def _attn_call(qt, k, vt, lq1, lk1, lq2, lk2, subln_g_col, lambda_init):
    b, nh, s, _ = k.shape
    tile = qt.shape[-1]
    vec = lambda n: pl.BlockSpec((1, n), lambda bi, hi: (0, 0))
    return pl.pallas_call(
        functools.partial(_attn_kernel, tile=tile, lambda_init=lambda_init),
        grid=(b, nh),
        in_specs=[
            pl.BlockSpec((1, 1, s // tile, HEAD_W, tile), lambda bi, hi: (bi, hi, 0, 0, 0)),
            pl.BlockSpec((1, 1, s, HEAD_W), lambda bi, hi: (bi, hi, 0, 0)),
            pl.BlockSpec((1, 1, s // tile, V_ROWS, tile), lambda bi, hi: (bi, hi, 0, 0, 0)),
            vec(HEAD_DIM), vec(HEAD_DIM), vec(HEAD_DIM), vec(HEAD_DIM),
            pl.BlockSpec((HEAD_W, 1), lambda bi, hi: (0, 0)),
        ],
        out_specs=pl.BlockSpec((1, s, HEAD_W), lambda bi, hi: (bi, 0, hi)),
        out_shape=jax.ShapeDtypeStruct((b, s, nh * HEAD_W), BF16),
        scratch_shapes=[
            pltpu.VMEM((HEAD_W, 2 * tile), BF16),
            pltpu.VMEM((1, 2 * tile), F32),
            pltpu.VMEM((2 * tile // LANES, V_ROWS, LANES), F32),
            pltpu.VMEM((2 * tile // LANES, tile, LANES), F32),
            pltpu.VMEM((2 * tile // LANES, tile, LANES), F32),
            pltpu.VMEM((1, 2 * tile), F32), pltpu.VMEM((1, 2 * tile), F32),
        ],
        compiler_params=pltpu.CompilerParams(
            dimension_semantics=("arbitrary", "arbitrary"),
            vmem_limit_bytes=V7X_VMEM_LIMIT_BYTES),
        name="diff_attn",
    )(qt, k, vt, lq1, lk1, lq2, lk2, subln_g_col)


def _mixer_kernel(x_ref, mod_ref, h_ref, wg_ref, o_ref, u_ref, uh_ref, cw_ref, cb_ref, lg_ref, lb_ref,
                  wao_ref, wco_ref, wout_ref, x1_ref, ubuf, *, tm):
    i = pl.program_id(1)
    d = x_ref.shape[-1]
    x = x_ref[0]
    h = h_ref[0]
    halo = uh_ref[0]
    ubuf[0:CONV_HALO, :] = jnp.where(i > 0, halo, jnp.zeros_like(halo))
    ubuf[CONV_HALO:CONV_HALO + tm, :] = u_ref[0]
    ubuf[CONV_HALO + tm:CONV_HALO + tm + SUBLANES, :] = jnp.zeros((SUBLANES, CONV_W), F32)
    base = CONV_HALO - (CONV_K - 1)

    def conv_lanes(lanes):
        out = cb_ref[:, lanes]
        for r in range(SUBLANES):
            part = None
            for a in range((base + CONV_K - 1 - r) // SUBLANES + 1):
                j = SUBLANES * a + r - base
                if j < 0:
                    continue
                term = cw_ref[j:j + 1, lanes] * ubuf[SUBLANES * a:SUBLANES * a + tm + SUBLANES, lanes]
                part = term if part is None else part + term
            out = out + part[r:r + tm, :]
        return out

    n_chunks = CONV_W // LANES
    gate_cols = wg_ref.shape[1] // n_chunks
    gate_parts, conv_parts = [], []
    for c in range(n_chunks):
        z = jnp.dot(h, wg_ref[:, c * gate_cols:(c + 1) * gate_cols], preferred_element_type=F32)
        gate_parts.append(_sigmoid(z))
        conv_parts.append(conv_lanes(slice(c * LANES, (c + 1) * LANES)))
    gates = jnp.concatenate(gate_parts, axis=1)
    conv = jnp.concatenate(conv_parts, axis=1)

    mu = jnp.mean(conv, axis=-1, keepdims=True)
    cen = conv - mu
    var = jnp.mean(cen * cen, axis=-1, keepdims=True)
    y = (cen * lax.rsqrt(var + 1e-5)) * lg_ref[...] + lb_ref[...]
    act = (y * _sigmoid(y)).astype(BF16)

    y_conv = jnp.dot(act, wco_ref[...], preferred_element_type=F32)
    y_attn = jnp.dot(o_ref[0], wao_ref[...], preferred_element_type=F32)
    mix = (gates[:, 0:d] * y_attn + gates[:, d:2 * d] * y_conv).astype(BF16)
    mixed = jnp.dot(mix, wout_ref[...], preferred_element_type=F32)
    x1_ref[0] = x + (1.0 + mod_ref[0, 2:3, :]) * mixed


def _mixer_call(x, mod, h, wg, o, u, cw, cb, lg, lb, wao, wco, wout):
    b, s, d = x.shape
    tm = min(ROW_TILE, s)
    halo_blocks = tm // CONV_HALO
    row = lambda bi, i: (bi, i, 0)
    return pl.pallas_call(
        functools.partial(_mixer_kernel, tm=tm),
        grid=(b, s // tm),
        in_specs=[
            pl.BlockSpec((1, tm, d), row),
            pl.BlockSpec((1, N_MOD, d), lambda bi, i: (bi, 0, 0)),
            pl.BlockSpec((1, tm, d), row),
            _resident(wg.shape),
            pl.BlockSpec((1, tm, ATTN_W), row),
            pl.BlockSpec((1, tm, CONV_W), row),
            pl.BlockSpec((1, CONV_HALO, CONV_W), lambda bi, i: (bi, jnp.maximum(i * halo_blocks - 1, 0), 0)),
            _resident(cw.shape), _resident(cb.shape), _resident(lg.shape), _resident(lb.shape),
            _resident(wao.shape), _resident(wco.shape), _resident(wout.shape),
        ],
        out_specs=pl.BlockSpec((1, tm, d), row),
        out_shape=jax.ShapeDtypeStruct((b, s, d), F32),
        scratch_shapes=[pltpu.VMEM((CONV_HALO + tm + SUBLANES, CONV_W), F32)],
        compiler_params=pltpu.CompilerParams(
            dimension_semantics=("arbitrary", "arbitrary"),
            vmem_limit_bytes=V7X_VMEM_LIMIT_BYTES),
        name="mixer",
    )(x, mod, h, wg, o, u, u, cw, cb, lg, lb, wao, wco, wout)


def _ffn_kernel(x_ref, mod_ref, g_ref, wi_ref, wo_ref, fg_ref, out_ref, *, d_ff, final):
    x = x_ref[0]
    h = _modulated_rms_norm(x, g_ref[...], mod_ref[0, 4:5, :], mod_ref[0, 3:4, :]).astype(BF16)
    acc = jnp.zeros(x.shape, F32)
    for c0 in range(0, d_ff, FFN_CHUNK):
        cw = min(FFN_CHUNK, d_ff - c0)
        f_gate = jnp.dot(h, wi_ref[:, c0:c0 + cw], preferred_element_type=F32)
        f_up = jnp.dot(h, wi_ref[:, d_ff + c0:d_ff + c0 + cw], preferred_element_type=F32)
        a = ((f_gate * _sigmoid(f_gate)) * f_up).astype(BF16)
        acc = acc + jnp.dot(a, wo_ref[c0:c0 + cw, :], preferred_element_type=F32)
    x2 = x + (1.0 + mod_ref[0, 5:6, :]) * acc
    if final:
        x2 = (x2 * lax.rsqrt(jnp.mean(x2 * x2, axis=-1, keepdims=True) + 1e-6)) * fg_ref[...]
    out_ref[0] = x2


def _ffn_call(x, mod, g, wi, wo, final_g, final):
    b, s, d = x.shape
    d_ff = wo.shape[0]
    tm = min(FFN_ROW_TILE, s)
    row = lambda bi, i: (bi, i, 0)
    return pl.pallas_call(
        functools.partial(_ffn_kernel, d_ff=d_ff, final=final),
        grid=(b, s // tm),
        in_specs=[
            pl.BlockSpec((1, tm, d), row),
            pl.BlockSpec((1, N_MOD, d), lambda bi, i: (bi, 0, 0)),
            _resident((1, d)),
            _resident(wi.shape),
            _resident(wo.shape),
            _resident((1, d)),
        ],
        out_specs=pl.BlockSpec((1, tm, d), row),
        out_shape=jax.ShapeDtypeStruct((b, s, d), F32),
        compiler_params=pltpu.CompilerParams(
            dimension_semantics=("arbitrary", "arbitrary"),
            vmem_limit_bytes=V7X_VMEM_LIMIT_BYTES),
        name="ffn_final" if final else "ffn",
    )(x, mod, g, wi, wo, final_g)


def _rope_tables(positions):
    inv_freq = 1.0 / (ROPE_THETA ** (jnp.arange(0, ROT_DIM, 2, dtype=F32) / ROT_DIM))
    ang = positions.astype(F32)[..., None] * inv_freq
    return jnp.swapaxes(jnp.cos(ang), 1, 2), jnp.swapaxes(jnp.sin(ang), 1, 2)


def kernel(x, c, positions, ada_w, ada_b, norm1_g, w_in, lambda_q1, lambda_k1, lambda_q2, lambda_k2,
           subln_g, w_attn_o, dw_conv_w, dw_conv_b, conv_ln_g, conv_ln_b, w_conv_o, w_out, norm2_g,
           w_ffn_in, w_ffn_out, final_g):
    b, s, d = x.shape
    depth = ada_w.shape[0]
    assert all(s % min(t, s) == 0 for t in (ROW_TILE, FFN_ROW_TILE, ATTN_TILE))
    assert (2 * min(ATTN_TILE, s)) % ATTN_GROUP == 0
    assert ada_w.shape[2] == N_MOD * d and ada_w.shape[2] % ADA_COLS == 0
    assert w_in.shape[2] == N_QKVU + 2 * d

    rows = -(-b // 8) * 8
    c_pad = jnp.zeros((rows, d), F32).at[:b].set(c)
    mod_all = _ada_call(c_pad, ada_w, ada_b)[:, :b].reshape(depth, b, N_MOD, d)

    cos_t, sin_t = _rope_tables(positions)
    final_row = final_g.reshape(1, d)

    for l in range(depth):
        lambda_init = 0.8 - 0.6 * math.exp(-0.3 * l)
        mod = mod_all[l]
        g1 = norm1_g[l].reshape(1, d)
        w_l = w_in[l]
        qt, k, vt, u, h = _inproj_call(x, mod, g1, w_l[:, :3 * ATTN_W].T.astype(BF16),
                                    w_l[:, 3 * ATTN_W:N_QKVU].astype(BF16), cos_t, sin_t)
        o = _attn_call(qt, k, vt,
                       lambda_q1[l].reshape(1, HEAD_DIM), lambda_k1[l].reshape(1, HEAD_DIM),
                       lambda_q2[l].reshape(1, HEAD_DIM), lambda_k2[l].reshape(1, HEAD_DIM),
                       subln_g[l].reshape(HEAD_W, 1), lambda_init)
        x = _mixer_call(x, mod, h, w_l[:, N_QKVU:].astype(BF16), o, u,
                        dw_conv_w[l].reshape(CONV_K, CONV_W), dw_conv_b[l].reshape(1, CONV_W),
                        conv_ln_g[l].reshape(1, CONV_W), conv_ln_b[l].reshape(1, CONV_W),
                        w_attn_o[l].astype(BF16), w_conv_o[l].astype(BF16), w_out[l].astype(BF16))
        x = _ffn_call(x, mod, norm2_g[l].reshape(1, d), w_ffn_in[l].astype(BF16),
                      w_ffn_out[l].astype(BF16), final_row, final=(l == depth - 1))
    return x
```

```python
import collections
import functools
import math

import jax
import jax.numpy as jnp
from jax import lax
from jax.experimental import pallas as pl
from jax.experimental.pallas import tpu as pltpu

F32 = jnp.float32
BF16 = jnp.bfloat16

ATTN_HEADS = 4
HEAD_DIM = 64
HEAD_W = 2 * HEAD_DIM
ATTN_W = ATTN_HEADS * HEAD_W
ROT_DIM = HEAD_DIM // 4
ROPE_THETA = 500000.0
CONV_W = 512
CONV_K = 31
N_MOD = 6
N_QKVU = 3 * ATTN_W + 2 * CONV_W
CONV_HALO = 32
MASK_VALUE = -1e30
Q_SCALE = math.log2(math.e) / math.sqrt(HEAD_DIM)

V7X_VMEM_LIMIT_BYTES = 56 * 1024 * 1024
V7X_MXU_COLS = 256
SUBLANES = 8
BF16_SUBLANES = 2 * SUBLANES
LANES = 128
V_ROWS = HEAD_W + BF16_SUBLANES

ROW_TILE = 1024
FFN_ROW_TILE = 1024
INPROJ_ROW_TILE = 1024
ATTN_TILE = 512
ATTN_GROUP = V7X_MXU_COLS
ADA_COLS = 1024
FFN_CHUNK = 1024


def _sigmoid(x):
    return 0.5 * jnp.tanh(0.5 * x) + 0.5


def _modulated_rms_norm(x, g, scale, shift, eps=1e-6):
    y = x * lax.rsqrt(jnp.mean(x * x, axis=-1, keepdims=True) + eps)
    return (y * g) * (1.0 + scale) + shift


def _resident(shape):
    zeros = (0,) * len(shape)
    return pl.BlockSpec(shape, lambda *_: zeros, pipeline_mode=pl.Buffered(1))


def _ada_kernel(c_ref, w_ref, b_ref, o_ref):
    c = c_ref[...]
    c_act = (c * _sigmoid(c)).astype(BF16)
    w = w_ref[0].astype(BF16)
    o_ref[0] = jnp.dot(c_act, w, preferred_element_type=F32) + b_ref[0]


def _ada_call(c_pad, ada_w, ada_b):
    depth, d, n = ada_w.shape
    rows = c_pad.shape[0]
    return pl.pallas_call(
        _ada_kernel,
        grid=(depth, n // ADA_COLS),
        in_specs=[
            pl.BlockSpec((rows, d), lambda l, j: (0, 0)),
            pl.BlockSpec((1, d, ADA_COLS), lambda l, j: (l, 0, j)),
            pl.BlockSpec((1, 1, ADA_COLS), lambda l, j: (l, 0, j)),
        ],
        out_specs=pl.BlockSpec((1, rows, ADA_COLS), lambda l, j: (l, 0, j)),
        out_shape=jax.ShapeDtypeStruct((depth, rows, n), F32),
        compiler_params=pltpu.CompilerParams(
            dimension_semantics=("arbitrary", "arbitrary"),
            vmem_limit_bytes=V7X_VMEM_LIMIT_BYTES),
        name="ada_proj",
    )(c_pad, ada_w, ada_b.reshape(depth, 1, n))


_NT_DIMS = (((1,), (1,)), ((), ()))


def _rope_rows(z, cos, sin):
    half = ROT_DIM // 2
    parts = []
    for comp in range(2):
        r0 = comp * HEAD_DIM
        x1 = z[r0:r0 + half]
        x2 = z[r0 + half:r0 + ROT_DIM]
        parts += [x1 * cos - x2 * sin, x2 * cos + x1 * sin, z[r0 + ROT_DIM:r0 + HEAD_DIM]]
    return jnp.concatenate(parts, axis=0)


def _inproj_kernel(x_ref, mod_ref, g_ref, wt_ref, wc_ref, cos_ref, sin_ref,
                   qt_ref, k_ref, vt_ref, u_ref, h_ref):
    n_attn, tile = qt_ref.shape[2], qt_ref.shape[4]
    x = x_ref[0]
    h = _modulated_rms_norm(x, g_ref[...], mod_ref[0, 1:2, :], mod_ref[0, 0:1, :]).astype(BF16)
    h_ref[0] = h
    cos = cos_ref[0]
    sin = sin_ref[0]

    zq = lax.dot_general(wt_ref[0:ATTN_W, :], h, _NT_DIMS, preferred_element_type=F32)
    for hd in range(ATTN_HEADS):
        t = (_rope_rows(zq[hd * HEAD_W:(hd + 1) * HEAD_W], cos, sin) * Q_SCALE).astype(BF16)
        for a in range(n_attn):
            qt_ref[0, hd, a] = t[:, a * tile:(a + 1) * tile]

    zk = lax.dot_general(wt_ref[ATTN_W:2 * ATTN_W, :], h, _NT_DIMS, preferred_element_type=F32)
    for hd in range(ATTN_HEADS):
        k_ref[0, hd] = _rope_rows(zk[hd * HEAD_W:(hd + 1) * HEAD_W], cos, sin).T.astype(BF16)

    zv = lax.dot_general(wt_ref[2 * ATTN_W:3 * ATTN_W, :], h, _NT_DIMS, preferred_element_type=F32)
    for hd in range(ATTN_HEADS):
        t = zv[hd * HEAD_W:(hd + 1) * HEAD_W].astype(BF16)
        for a in range(n_attn):
            vt_ref[0, hd, a, 0:HEAD_W, :] = t[:, a * tile:(a + 1) * tile]
            vt_ref[0, hd, a, HEAD_W:V_ROWS, :] = jnp.ones((V_ROWS - HEAD_W, tile), BF16)

    ga = jnp.dot(h, wc_ref[:, 0:CONV_W], preferred_element_type=F32)
    gb = jnp.dot(h, wc_ref[:, CONV_W:2 * CONV_W], preferred_element_type=F32)
    u_ref[0] = ga * _sigmoid(gb)


def _inproj_call(x, mod, g, wt, wc, cos_t, sin_t):
    b, s, d = x.shape
    tile = min(ATTN_TILE, s)
    tm = min(INPROJ_ROW_TILE, s)
    row = lambda bi, i: (bi, i, 0)
    t_spec = lambda rows: pl.BlockSpec((1, ATTN_HEADS, tm // tile, rows, tile), lambda bi, i: (bi, 0, i, 0, 0))
    t_shape = lambda rows: jax.ShapeDtypeStruct((b, ATTN_HEADS, s // tile, rows, tile), BF16)
    rot_spec = pl.BlockSpec((1, ROT_DIM // 2, tm), lambda bi, i: (bi, 0, i))
    return pl.pallas_call(
        _inproj_kernel,
        grid=(b, s // tm),
        in_specs=[
            pl.BlockSpec((1, tm, d), row),
            pl.BlockSpec((1, N_MOD, d), lambda bi, i: (bi, 0, 0)),
            _resident((1, d)),
            _resident(wt.shape),
            _resident(wc.shape),
            rot_spec, rot_spec,
        ],
        out_specs=[
            t_spec(HEAD_W),
            pl.BlockSpec((1, ATTN_HEADS, tm, HEAD_W), lambda bi, i: (bi, 0, i, 0)),
            t_spec(V_ROWS),
            pl.BlockSpec((1, tm, CONV_W), row),
            pl.BlockSpec((1, tm, d), row),
        ],
        out_shape=[
            t_shape(HEAD_W),
            jax.ShapeDtypeStruct((b, ATTN_HEADS, s, HEAD_W), BF16),
            t_shape(V_ROWS),
            jax.ShapeDtypeStruct((b, s, CONV_W), F32),
            jax.ShapeDtypeStruct((b, s, d), BF16),
        ],
        compiler_params=pltpu.CompilerParams(
            dimension_semantics=("arbitrary", "arbitrary"),
            vmem_limit_bytes=V7X_VMEM_LIMIT_BYTES),
        name="in_proj",
    )(x, mod, g, wt, wc, cos_t, sin_t)


_StageBuffers = collections.namedtuple("_StageBuffers", ["s", "smax"])


def _attn_kernel(qt_ref, k_ref, vt_ref, lq1_ref, lk1_ref, lq2_ref, lk2_ref, g_ref, o_ref,
                 qbd_scr, m_scr, acc_scr, s0, s1, x0, x1, *, tile, lambda_init):
    n_tiles = qt_ref.shape[2]
    groups = [slice(c0, c0 + ATTN_GROUP) for c0 in range(0, 2 * tile, ATTN_GROUP)]
    even = _StageBuffers(s0, x0)
    odd = _StageBuffers(s1, x1)

    def load_queries(i):
        qt = qt_ref[0, 0, i]
        first = lax.broadcasted_iota(jnp.int32, qt.shape, 0) < HEAD_DIM
        zero = jnp.zeros_like(qt)
        qbd_scr[:, 0:tile] = jnp.where(first, qt, zero)
        qbd_scr[:, tile:2 * tile] = jnp.where(first, zero, qt)

    def read_cols(ref, cols, rows=slice(None)):
        return jnp.concatenate([ref[t, rows, :] for t in range(cols.start // LANES, cols.stop // LANES)], axis=1)

    def write_cols(ref, cols, value):
        for n, t in enumerate(range(cols.start // LANES, cols.stop // LANES)):
            ref[t] = value[:, n * LANES:(n + 1) * LANES]

    def scores(j, buf, cols):
        start = pl.multiple_of(j * tile, tile)
        kb = k_ref[0, 0, pl.ds(start, tile), :]
        s = jnp.dot(kb, qbd_scr[:, cols], preferred_element_type=F32)
        write_cols(buf.s, cols, s)
        buf.smax[:, cols] = jnp.max(s, axis=0, keepdims=True)

    def softmax(j, buf, cols, masked):
        if masked:
            first_query = cols.start % tile
            keys = min(tile, first_query + ATTN_GROUP)
            s = read_cols(buf.s, cols, slice(0, keys))
            key = lax.broadcasted_iota(jnp.int32, s.shape, 0)
            query = lax.broadcasted_iota(jnp.int32, s.shape, 1) + first_query
            s = jnp.where(key <= query, s, MASK_VALUE)
            block_max = jnp.max(s, axis=0, keepdims=True)
        else:
            keys = tile
            s = read_cols(buf.s, cols)
            block_max = buf.smax[:, cols]
        m_prev = m_scr[:, cols]
        m_new = jnp.maximum(m_prev, block_max)
        p = jnp.exp2(s - m_new).astype(BF16)
        alpha = jnp.exp2(m_prev - m_new)
        m_scr[:, cols] = m_new
        write_cols(acc_scr, cols, alpha * read_cols(acc_scr, cols) + jnp.dot(
            vt_ref[0, 0, j, :, 0:keys], p, preferred_element_type=F32))

    def substep(j, cur, other):
        for cols in groups:
            scores(j + 1, other, cols)
            softmax(j, cur, cols, False)

    def last_block(i, cur):
        for cols in groups:
            softmax(i, cur, cols, True)
            scores(0, even, cols)

    lam = (jnp.exp(jnp.sum(lq1_ref[...] * lk1_ref[...], axis=-1, keepdims=True))
           - jnp.exp(jnp.sum(lq2_ref[...] * lk2_ref[...], axis=-1, keepdims=True))
           + lambda_init)

    def finish(i):
        acc1 = read_cols(acc_scr, slice(0, tile))
        acc2 = read_cols(acc_scr, slice(tile, 2 * tile))
        o1 = acc1[0:HEAD_W] / acc1[HEAD_W:HEAD_W + 1]
        o2 = acc2[0:HEAD_W] / acc2[HEAD_W:HEAD_W + 1]
        o = o1 - lam * o2
        o = o * lax.rsqrt(jnp.mean(o * o, axis=0, keepdims=True) + 1e-5)
        o = (o * g_ref[...]) * (1.0 - lambda_init)
        o_ref[0, pl.ds(pl.multiple_of(i * tile, tile), tile), :] = o.T.astype(BF16)

    def query_tile(i, carry):
        m_scr[...] = jnp.full(m_scr.shape, MASK_VALUE, F32)
        acc_scr[...] = jnp.zeros(acc_scr.shape, F32)
        i_next = jnp.minimum(i + 1, n_tiles - 1)

        parity = (even, odd)

        def quad(jj, c):
            for n in range(4):
                substep(4 * jj + n, parity[n % 2], parity[(n + 1) % 2])
            return c

        lax.fori_loop(0, lax.shift_right_logical(i, 2), quad, 0)

        def tail(leftover):
            for n in range(leftover):
                substep(i - leftover + n, parity[n % 2], parity[(n + 1) % 2])
            load_queries(i_next)
            last_block(i, parity[leftover % 2])
            finish(i)

        for leftover in range(4):
            pl.when((i & 3) == leftover)(functools.partial(tail, leftover))
        return carry

    load_queries(0)
    for cols in groups:
        scores(0, even, cols)
    lax.fori_loop(0, n_tiles, query_tile, 0)


def _attn_call(qt, k, vt, lq1, lk1, lq2, lk2, subln_g_col, lambda_init):
    b, nh, s, _ = k.shape
    tile = qt.shape[-1]
    vec = lambda n: pl.BlockSpec((1, n), lambda bi, hi: (0, 0))
    return pl.pallas_call(
        functools.partial(_attn_kernel, tile=tile, lambda_init=lambda_init),
        grid=(b, nh),
        in_specs=[
            pl.BlockSpec((1, 1, s // tile, HEAD_W, tile), lambda bi, hi: (bi, hi, 0, 0, 0)),
            pl.BlockSpec((1, 1, s, HEAD_W), lambda bi, hi: (bi, hi, 0, 0)),
            pl.BlockSpec((1, 1, s // tile, V_ROWS, tile), lambda bi, hi: (bi, hi, 0, 0, 0)),
            vec(HEAD_DIM), vec(HEAD_DIM), vec(HEAD_DIM), vec(HEAD_DIM),
            pl.BlockSpec((HEAD_W, 1), lambda bi, hi: (0, 0)),
        ],
        out_specs=pl.BlockSpec((1, s, HEAD_W), lambda bi, hi: (bi, 0, hi)),
        out_shape=jax.ShapeDtypeStruct((b, s, nh * HEAD_W), BF16),
        scratch_shapes=[
            pltpu.VMEM((HEAD_W, 2 * tile), BF16),
            pltpu.VMEM((1, 2 * tile), F32),
            pltpu.VMEM((2 * tile // LANES, V_ROWS, LANES), F32),
            pltpu.VMEM((2 * tile // LANES, tile, LANES), F32),
            pltpu.VMEM((2 * tile // LANES, tile, LANES), F32),
            pltpu.VMEM((1, 2 * tile), F32), pltpu.VMEM((1, 2 * tile), F32),
        ],
        compiler_params=pltpu.CompilerParams(
            dimension_semantics=("arbitrary", "arbitrary"),
            vmem_limit_bytes=V7X_VMEM_LIMIT_BYTES),
        name="diff_attn",
    )(qt, k, vt, lq1, lk1, lq2, lk2, subln_g_col)


def _mixer_kernel(x_ref, mod_ref, h_ref, wg_ref, o_ref, u_ref, uh_ref, cw_ref, cb_ref, lg_ref, lb_ref,
                  wao_ref, wco_ref, wout_ref, x1_ref, ubuf, *, tm):
    i = pl.program_id(1)
    d = x_ref.shape[-1]
    x = x_ref[0]
    h = h_ref[0]
    halo = uh_ref[0]
    ubuf[0:CONV_HALO, :] = jnp.where(i > 0, halo, jnp.zeros_like(halo))
    ubuf[CONV_HALO:CONV_HALO + tm, :] = u_ref[0]
    ubuf[CONV_HALO + tm:CONV_HALO + tm + SUBLANES, :] = jnp.zeros((SUBLANES, CONV_W), F32)
    base = CONV_HALO - (CONV_K - 1)

    def conv_lanes(lanes):
        out = cb_ref[:, lanes]
        for r in range(SUBLANES):
            part = None
            for a in range((base + CONV_K - 1 - r) // SUBLANES + 1):
                j = SUBLANES * a + r - base
                if j < 0:
                    continue
                term = cw_ref[j:j + 1, lanes] * ubuf[SUBLANES * a:SUBLANES * a + tm + SUBLANES, lanes]
                part = term if part is None else part + term
            out = out + part[r:r + tm, :]
        return out

    n_chunks = CONV_W // LANES
    gate_cols = wg_ref.shape[1] // n_chunks
    gate_parts, conv_parts = [], []
    for c in range(n_chunks):
        z = jnp.dot(h, wg_ref[:, c * gate_cols:(c + 1) * gate_cols], preferred_element_type=F32)
        gate_parts.append(_sigmoid(z))
        conv_parts.append(conv_lanes(slice(c * LANES, (c + 1) * LANES)))
    gates = jnp.concatenate(gate_parts, axis=1)
    conv = jnp.concatenate(conv_parts, axis=1)

    mu = jnp.mean(conv, axis=-1, keepdims=True)
    cen = conv - mu
    var = jnp.mean(cen * cen, axis=-1, keepdims=True)
    y = (cen * lax.rsqrt(var + 1e-5)) * lg_ref[...] + lb_ref[...]
    act = (y * _sigmoid(y)).astype(BF16)

    y_conv = jnp.dot(act, wco_ref[...], preferred_element_type=F32)
    y_attn = jnp.dot(o_ref[0], wao_ref[...], preferred_element_type=F32)
    mix = (gates[:, 0:d] * y_attn + gates[:, d:2 * d] * y_conv).astype(BF16)
    mixed = jnp.dot(mix, wout_ref[...], preferred_element_type=F32)
    x1_ref[0] = x + (1.0 + mod_ref[0, 2:3, :]) * mixed


def _mixer_call(x, mod, h, wg, o, u, cw, cb, lg, lb, wao, wco, wout):
    b, s, d = x.shape
    tm = min(ROW_TILE, s)
    halo_blocks = tm // CONV_HALO
    row = lambda bi, i: (bi, i, 0)
    return pl.pallas_call(
        functools.partial(_mixer_kernel, tm=tm),
        grid=(b, s // tm),
        in_specs=[
            pl.BlockSpec((1, tm, d), row),
            pl.BlockSpec((1, N_MOD, d), lambda bi, i: (bi, 0, 0)),
            pl.BlockSpec((1, tm, d), row),
            _resident(wg.shape),
            pl.BlockSpec((1, tm, ATTN_W), row),
            pl.BlockSpec((1, tm, CONV_W), row),
            pl.BlockSpec((1, CONV_HALO, CONV_W), lambda bi, i: (bi, jnp.maximum(i * halo_blocks - 1, 0), 0)),
            _resident(cw.shape), _resident(cb.shape), _resident(lg.shape), _resident(lb.shape),
            _resident(wao.shape), _resident(wco.shape), _resident(wout.shape),
        ],
        out_specs=pl.BlockSpec((1, tm, d), row),
        out_shape=jax.ShapeDtypeStruct((b, s, d), F32),
        scratch_shapes=[pltpu.VMEM((CONV_HALO + tm + SUBLANES, CONV_W), F32)],
        compiler_params=pltpu.CompilerParams(
            dimension_semantics=("arbitrary", "arbitrary"),
            vmem_limit_bytes=V7X_VMEM_LIMIT_BYTES),
        name="mixer",
    )(x, mod, h, wg, o, u, u, cw, cb, lg, lb, wao, wco, wout)


def _ffn_kernel(x_ref, mod_ref, g_ref, wi_ref, wo_ref, fg_ref, out_ref, *, d_ff, final):
    x = x_ref[0]
    h = _modulated_rms_norm(x, g_ref[...], mod_ref[0, 4:5, :], mod_ref[0, 3:4, :]).astype(BF16)
    acc = jnp.zeros(x.shape, F32)
    for c0 in range(0, d_ff, FFN_CHUNK):
        cw = min(FFN_CHUNK, d_ff - c0)
        f_gate = jnp.dot(h, wi_ref[:, c0:c0 + cw], preferred_element_type=F32)
        f_up = jnp.dot(h, wi_ref[:, d_ff + c0:d_ff + c0 + cw], preferred_element_type=F32)
        a = ((f_gate * _sigmoid(f_gate)) * f_up).astype(BF16)
        acc = acc + jnp.dot(a, wo_ref[c0:c0 + cw, :], preferred_element_type=F32)
    x2 = x + (1.0 + mod_ref[0, 5:6, :]) * acc
    if final:
        x2 = (x2 * lax.rsqrt(jnp.mean(x2 * x2, axis=-1, keepdims=True) + 1e-6)) * fg_ref[...]
    out_ref[0] = x2


def _ffn_call(x, mod, g, wi, wo, final_g, final):
    b, s, d = x.shape
    d_ff = wo.shape[0]
    tm = min(FFN_ROW_TILE, s)
    row = lambda bi, i: (bi, i, 0)
    return pl.pallas_call(
        functools.partial(_ffn_kernel, d_ff=d_ff, final=final),
        grid=(b, s // tm),
        in_specs=[
            pl.BlockSpec((1, tm, d), row),
            pl.BlockSpec((1, N_MOD, d), lambda bi, i: (bi, 0, 0)),
            _resident((1, d)),
            _resident(wi.shape),
            _resident(wo.shape),
            _resident((1, d)),
        ],
        out_specs=pl.BlockSpec((1, tm, d), row),
        out_shape=jax.ShapeDtypeStruct((b, s, d), F32),
        compiler_params=pltpu.CompilerParams(
            dimension_semantics=("arbitrary", "arbitrary"),
            vmem_limit_bytes=V7X_VMEM_LIMIT_BYTES),
        name="ffn_final" if final else "ffn",
    )(x, mod, g, wi, wo, final_g)


def _rope_tables(positions):
    inv_freq = 1.0 / (ROPE_THETA ** (jnp.arange(0, ROT_DIM, 2, dtype=F32) / ROT_DIM))
    ang = positions.astype(F32)[..., None] * inv_freq
    return jnp.swapaxes(jnp.cos(ang), 1, 2), jnp.swapaxes(jnp.sin(ang), 1, 2)


def kernel(x, c, positions, ada_w, ada_b, norm1_g, w_in, lambda_q1, lambda_k1, lambda_q2, lambda_k2,
           subln_g, w_attn_o, dw_conv_w, dw_conv_b, conv_ln_g, conv_ln_b, w_conv_o, w_out, norm2_g,
           w_ffn_in, w_ffn_out, final_g):
    b, s, d = x.shape
    depth = ada_w.shape[0]
    assert all(s % min(t, s) == 0 for t in (ROW_TILE, FFN_ROW_TILE, INPROJ_ROW_TILE, ATTN_TILE))
    assert min(INPROJ_ROW_TILE, s) % min(ATTN_TILE, s) == 0
    assert (2 * min(ATTN_TILE, s)) % ATTN_GROUP == 0
    assert ada_w.shape[2] == N_MOD * d and ada_w.shape[2] % ADA_COLS == 0
    assert w_in.shape[2] == N_QKVU + 2 * d

    rows = -(-b // 8) * 8
    c_pad = jnp.zeros((rows, d), F32).at[:b].set(c)
    mod_all = _ada_call(c_pad, ada_w, ada_b)[:, :b].reshape(depth, b, N_MOD, d)

    cos_t, sin_t = _rope_tables(positions)
    final_row = final_g.reshape(1, d)

    for l in range(depth):
        lambda_init = 0.8 - 0.6 * math.exp(-0.3 * l)
        mod = mod_all[l]
        g1 = norm1_g[l].reshape(1, d)
        w_l = w_in[l]
        qt, k, vt, u, h = _inproj_call(x, mod, g1, w_l[:, :3 * ATTN_W].T.astype(BF16),
                                    w_l[:, 3 * ATTN_W:N_QKVU].astype(BF16), cos_t, sin_t)
        o = _attn_call(qt, k, vt,
                       lambda_q1[l].reshape(1, HEAD_DIM), lambda_k1[l].reshape(1, HEAD_DIM),
                       lambda_q2[l].reshape(1, HEAD_DIM), lambda_k2[l].reshape(1, HEAD_DIM),
                       subln_g[l].reshape(HEAD_W, 1), lambda_init)
        x = _mixer_call(x, mod, h, w_l[:, N_QKVU:].astype(BF16), o, u,
                        dw_conv_w[l].reshape(CONV_K, CONV_W), dw_conv_b[l].reshape(1, CONV_W),
                        conv_ln_g[l].reshape(1, CONV_W), conv_ln_b[l].reshape(1, CONV_W),
                        w_attn_o[l].astype(BF16), w_conv_o[l].astype(BF16), w_out[l].astype(BF16))
        x = _ffn_call(x, mod, norm2_g[l].reshape(1, d), w_ffn_in[l].astype(BF16),
                      w_ffn_out[l].astype(BF16), final_row, final=(l == depth - 1))
    return x
```

```python
import collections
import functools
import math

import jax
import jax.numpy as jnp
from jax import lax
from jax.experimental import pallas as pl
from jax.experimental.pallas import tpu as pltpu

F32 = jnp.float32
BF16 = jnp.bfloat16

ATTN_HEADS = 4
HEAD_DIM = 64
HEAD_W = 2 * HEAD_DIM
ATTN_W = ATTN_HEADS * HEAD_W
ROT_DIM = HEAD_DIM // 4
ROPE_THETA = 500000.0
CONV_W = 512
CONV_K = 31
N_MOD = 6
N_QKVU = 3 * ATTN_W + 2 * CONV_W
CONV_HALO = 32
MASK_VALUE = -1e30
Q_SCALE = math.log2(math.e) / math.sqrt(HEAD_DIM)

V7X_VMEM_LIMIT_BYTES = 56 * 1024 * 1024
V7X_MXU_COLS = 256
SUBLANES = 8
BF16_SUBLANES = 2 * SUBLANES
LANES = 128
V_ROWS = HEAD_W + BF16_SUBLANES

ROW_TILE = 1024
FFN_ROW_TILE = 1024
INPROJ_ROW_TILE = 1024
ATTN_TILE = 1024
ATTN_GROUP = V7X_MXU_COLS
ADA_COLS = 1024
FFN_CHUNK = 1024


def _sigmoid(x):
    return 0.5 * jnp.tanh(0.5 * x) + 0.5


def _modulated_rms_norm(x, g, scale, shift, eps=1e-6):
    y = x * lax.rsqrt(jnp.mean(x * x, axis=-1, keepdims=True) + eps)
    return (y * g) * (1.0 + scale) + shift


def _resident(shape):
    zeros = (0,) * len(shape)
    return pl.BlockSpec(shape, lambda *_: zeros, pipeline_mode=pl.Buffered(1))


def _ada_kernel(c_ref, w_ref, b_ref, o_ref):
    c = c_ref[...]
    c_act = (c * _sigmoid(c)).astype(BF16)
    w = w_ref[0].astype(BF16)
    o_ref[0] = jnp.dot(c_act, w, preferred_element_type=F32) + b_ref[0]


def _ada_call(c_pad, ada_w, ada_b):
    depth, d, n = ada_w.shape
    rows = c_pad.shape[0]
    return pl.pallas_call(
        _ada_kernel,
        grid=(depth, n // ADA_COLS),
        in_specs=[
            pl.BlockSpec((rows, d), lambda l, j: (0, 0)),
            pl.BlockSpec((1, d, ADA_COLS), lambda l, j: (l, 0, j)),
            pl.BlockSpec((1, 1, ADA_COLS), lambda l, j: (l, 0, j)),
        ],
        out_specs=pl.BlockSpec((1, rows, ADA_COLS), lambda l, j: (l, 0, j)),
        out_shape=jax.ShapeDtypeStruct((depth, rows, n), F32),
        compiler_params=pltpu.CompilerParams(
            dimension_semantics=("arbitrary", "arbitrary"),
            vmem_limit_bytes=V7X_VMEM_LIMIT_BYTES),
        name="ada_proj",
    )(c_pad, ada_w, ada_b.reshape(depth, 1, n))


_NT_DIMS = (((1,), (1,)), ((), ()))


def _rope_rows(z, cos, sin):
    half = ROT_DIM // 2
    parts = []
    for comp in range(2):
        r0 = comp * HEAD_DIM
        x1 = z[r0:r0 + half]
        x2 = z[r0 + half:r0 + ROT_DIM]
        parts += [x1 * cos - x2 * sin, x2 * cos + x1 * sin, z[r0 + ROT_DIM:r0 + HEAD_DIM]]
    return jnp.concatenate(parts, axis=0)


def _inproj_kernel(x_ref, mod_ref, g_ref, wt_ref, wc_ref, cos_ref, sin_ref,
                   qt_ref, k_ref, vt_ref, u_ref, h_ref):
    n_attn, tile = qt_ref.shape[2], qt_ref.shape[4]
    x = x_ref[0]
    h = _modulated_rms_norm(x, g_ref[...], mod_ref[0, 1:2, :], mod_ref[0, 0:1, :]).astype(BF16)
    h_ref[0] = h
    cos = cos_ref[0]
    sin = sin_ref[0]

    zq = lax.dot_general(wt_ref[0:ATTN_W, :], h, _NT_DIMS, preferred_element_type=F32)
    for hd in range(ATTN_HEADS):
        t = (_rope_rows(zq[hd * HEAD_W:(hd + 1) * HEAD_W], cos, sin) * Q_SCALE).astype(BF16)
        for a in range(n_attn):
            qt_ref[0, hd, a] = t[:, a * tile:(a + 1) * tile]

    zk = lax.dot_general(wt_ref[ATTN_W:2 * ATTN_W, :], h, _NT_DIMS, preferred_element_type=F32)
    for hd in range(ATTN_HEADS):
        k_ref[0, hd] = _rope_rows(zk[hd * HEAD_W:(hd + 1) * HEAD_W], cos, sin).T.astype(BF16)

    zv = lax.dot_general(wt_ref[2 * ATTN_W:3 * ATTN_W, :], h, _NT_DIMS, preferred_element_type=F32)
    for hd in range(ATTN_HEADS):
        t = zv[hd * HEAD_W:(hd + 1) * HEAD_W].astype(BF16)
        for a in range(n_attn):
            vt_ref[0, hd, a, 0:HEAD_W, :] = t[:, a * tile:(a + 1) * tile]
            vt_ref[0, hd, a, HEAD_W:V_ROWS, :] = jnp.ones((V_ROWS - HEAD_W, tile), BF16)

    ga = jnp.dot(h, wc_ref[:, 0:CONV_W], preferred_element_type=F32)
    gb = jnp.dot(h, wc_ref[:, CONV_W:2 * CONV_W], preferred_element_type=F32)
    u_ref[0] = ga * _sigmoid(gb)


def _inproj_call(x, mod, g, wt, wc, cos_t, sin_t):
    b, s, d = x.shape
    tile = min(ATTN_TILE, s)
    tm = min(INPROJ_ROW_TILE, s)
    row = lambda bi, i: (bi, i, 0)
    t_spec = lambda rows: pl.BlockSpec((1, ATTN_HEADS, tm // tile, rows, tile), lambda bi, i: (bi, 0, i, 0, 0))
    t_shape = lambda rows: jax.ShapeDtypeStruct((b, ATTN_HEADS, s // tile, rows, tile), BF16)
    rot_spec = pl.BlockSpec((1, ROT_DIM // 2, tm), lambda bi, i: (bi, 0, i))
    return pl.pallas_call(
        _inproj_kernel,
        grid=(b, s // tm),
        in_specs=[
            pl.BlockSpec((1, tm, d), row),
            pl.BlockSpec((1, N_MOD, d), lambda bi, i: (bi, 0, 0)),
            _resident((1, d)),
            _resident(wt.shape),
            _resident(wc.shape),
            rot_spec, rot_spec,
        ],
        out_specs=[
            t_spec(HEAD_W),
            pl.BlockSpec((1, ATTN_HEADS, tm, HEAD_W), lambda bi, i: (bi, 0, i, 0)),
            t_spec(V_ROWS),
            pl.BlockSpec((1, tm, CONV_W), row),
            pl.BlockSpec((1, tm, d), row),
        ],
        out_shape=[
            t_shape(HEAD_W),
            jax.ShapeDtypeStruct((b, ATTN_HEADS, s, HEAD_W), BF16),
            t_shape(V_ROWS),
            jax.ShapeDtypeStruct((b, s, CONV_W), F32),
            jax.ShapeDtypeStruct((b, s, d), BF16),
        ],
        compiler_params=pltpu.CompilerParams(
            dimension_semantics=("arbitrary", "arbitrary"),
            vmem_limit_bytes=V7X_VMEM_LIMIT_BYTES),
        name="in_proj",
    )(x, mod, g, wt, wc, cos_t, sin_t)


_StageBuffers = collections.namedtuple("_StageBuffers", ["s", "smax"])


def _attn_kernel(qt_ref, k_ref, vt_ref, lq1_ref, lk1_ref, lq2_ref, lk2_ref, g_ref, o_ref,
                 qbd_scr, m_scr, acc_scr, s0, s1, x0, x1, *, tile, lambda_init):
    n_tiles = qt_ref.shape[2]
    groups = [slice(c0, c0 + ATTN_GROUP) for c0 in range(0, 2 * tile, ATTN_GROUP)]
    even = _StageBuffers(s0, x0)
    odd = _StageBuffers(s1, x1)

    def load_queries(i):
        qt = qt_ref[0, 0, i]
        first = lax.broadcasted_iota(jnp.int32, qt.shape, 0) < HEAD_DIM
        zero = jnp.zeros_like(qt)
        qbd_scr[:, 0:tile] = jnp.where(first, qt, zero)
        qbd_scr[:, tile:2 * tile] = jnp.where(first, zero, qt)

    def read_cols(ref, cols, rows=slice(None)):
        return jnp.concatenate([ref[t, rows, :] for t in range(cols.start // LANES, cols.stop // LANES)], axis=1)

    def write_cols(ref, cols, value):
        for n, t in enumerate(range(cols.start // LANES, cols.stop // LANES)):
            ref[t] = value[:, n * LANES:(n + 1) * LANES]

    def scores(j, buf, cols):
        start = pl.multiple_of(j * tile, tile)
        kb = k_ref[0, 0, pl.ds(start, tile), :]
        s = jnp.dot(kb, qbd_scr[:, cols], preferred_element_type=F32)
        write_cols(buf.s, cols, s)
        buf.smax[:, cols] = jnp.max(s, axis=0, keepdims=True)

    def softmax(j, buf, cols, masked):
        if masked:
            first_query = cols.start % tile
            keys = min(tile, first_query + ATTN_GROUP)
            s = read_cols(buf.s, cols, slice(0, keys))
            key = lax.broadcasted_iota(jnp.int32, s.shape, 0)
            query = lax.broadcasted_iota(jnp.int32, s.shape, 1) + first_query
            s = jnp.where(key <= query, s, MASK_VALUE)
            block_max = jnp.max(s, axis=0, keepdims=True)
        else:
            keys = tile
            s = read_cols(buf.s, cols)
            block_max = buf.smax[:, cols]
        m_prev = m_scr[:, cols]
        m_new = jnp.maximum(m_prev, block_max)
        p = jnp.exp2(s - m_new).astype(BF16)
        alpha = jnp.exp2(m_prev - m_new)
        m_scr[:, cols] = m_new
        write_cols(acc_scr, cols, alpha * read_cols(acc_scr, cols) + jnp.dot(
            vt_ref[0, 0, j, :, 0:keys], p, preferred_element_type=F32))

    def substep(j, cur, other):
        for cols in groups:
            scores(j + 1, other, cols)
            softmax(j, cur, cols, False)

    def last_block(i, cur):
        for cols in groups:
            softmax(i, cur, cols, True)
            scores(0, even, cols)

    lam = (jnp.exp(jnp.sum(lq1_ref[...] * lk1_ref[...], axis=-1, keepdims=True))
           - jnp.exp(jnp.sum(lq2_ref[...] * lk2_ref[...], axis=-1, keepdims=True))
           + lambda_init)

    def finish(i):
        acc1 = read_cols(acc_scr, slice(0, tile))
        acc2 = read_cols(acc_scr, slice(tile, 2 * tile))
        o1 = acc1[0:HEAD_W] / acc1[HEAD_W:HEAD_W + 1]
        o2 = acc2[0:HEAD_W] / acc2[HEAD_W:HEAD_W + 1]
        o = o1 - lam * o2
        o = o * lax.rsqrt(jnp.mean(o * o, axis=0, keepdims=True) + 1e-5)
        o = (o * g_ref[...]) * (1.0 - lambda_init)
        o_ref[0, pl.ds(pl.multiple_of(i * tile, tile), tile), :] = o.T.astype(BF16)

    def query_tile(i, carry):
        m_scr[...] = jnp.full(m_scr.shape, MASK_VALUE, F32)
        acc_scr[...] = jnp.zeros(acc_scr.shape, F32)
        i_next = jnp.minimum(i + 1, n_tiles - 1)

        parity = (even, odd)

        def quad(jj, c):
            for n in range(4):
                substep(4 * jj + n, parity[n % 2], parity[(n + 1) % 2])
            return c

        lax.fori_loop(0, lax.shift_right_logical(i, 2), quad, 0)

        def tail(leftover):
            for n in range(leftover):
                substep(i - leftover + n, parity[n % 2], parity[(n + 1) % 2])
            load_queries(i_next)
            last_block(i, parity[leftover % 2])
            finish(i)

        for leftover in range(4):
            pl.when((i & 3) == leftover)(functools.partial(tail, leftover))
        return carry

    load_queries(0)
    for cols in groups:
        scores(0, even, cols)
    lax.fori_loop(0, n_tiles, query_tile, 0)


def _attn_call(qt, k, vt, lq1, lk1, lq2, lk2, subln_g_col, lambda_init):
    b, nh, s, _ = k.shape
    tile = qt.shape[-1]
    vec = lambda n: pl.BlockSpec((1, n), lambda bi, hi: (0, 0))
    return pl.pallas_call(
        functools.partial(_attn_kernel, tile=tile, lambda_init=lambda_init),
        grid=(b, nh),
        in_specs=[
            pl.BlockSpec((1, 1, s // tile, HEAD_W, tile), lambda bi, hi: (bi, hi, 0, 0, 0)),
            pl.BlockSpec((1, 1, s, HEAD_W), lambda bi, hi: (bi, hi, 0, 0)),
            pl.BlockSpec((1, 1, s // tile, V_ROWS, tile), lambda bi, hi: (bi, hi, 0, 0, 0)),
            vec(HEAD_DIM), vec(HEAD_DIM), vec(HEAD_DIM), vec(HEAD_DIM),
            pl.BlockSpec((HEAD_W, 1), lambda bi, hi: (0, 0)),
        ],
        out_specs=pl.BlockSpec((1, s, HEAD_W), lambda bi, hi: (bi, 0, hi)),
        out_shape=jax.ShapeDtypeStruct((b, s, nh * HEAD_W), BF16),
        scratch_shapes=[
            pltpu.VMEM((HEAD_W, 2 * tile), BF16),
            pltpu.VMEM((1, 2 * tile), F32),
            pltpu.VMEM((2 * tile // LANES, V_ROWS, LANES), F32),
            pltpu.VMEM((2 * tile // LANES, tile, LANES), F32),
            pltpu.VMEM((2 * tile // LANES, tile, LANES), F32),
            pltpu.VMEM((1, 2 * tile), F32), pltpu.VMEM((1, 2 * tile), F32),
        ],
        compiler_params=pltpu.CompilerParams(
            dimension_semantics=("arbitrary", "arbitrary"),
            vmem_limit_bytes=V7X_VMEM_LIMIT_BYTES),
        name="diff_attn",
    )(qt, k, vt, lq1, lk1, lq2, lk2, subln_g_col)


def _mixer_kernel(x_ref, mod_ref, h_ref, wg_ref, o_ref, u_ref, uh_ref, cw_ref, cb_ref, lg_ref, lb_ref,
                  wao_ref, wco_ref, wout_ref, x1_ref, ubuf, *, tm):
    i = pl.program_id(1)
    d = x_ref.shape[-1]
    x = x_ref[0]
    h = h_ref[0]
    halo = uh_ref[0]
    ubuf[0:CONV_HALO, :] = jnp.where(i > 0, halo, jnp.zeros_like(halo))
    ubuf[CONV_HALO:CONV_HALO + tm, :] = u_ref[0]
    ubuf[CONV_HALO + tm:CONV_HALO + tm + SUBLANES, :] = jnp.zeros((SUBLANES, CONV_W), F32)
    base = CONV_HALO - (CONV_K - 1)

    def conv_lanes(lanes):
        out = cb_ref[:, lanes]
        for r in range(SUBLANES):
            part = None
            for a in range((base + CONV_K - 1 - r) // SUBLANES + 1):
                j = SUBLANES * a + r - base
                if j < 0:
                    continue
                term = cw_ref[j:j + 1, lanes] * ubuf[SUBLANES * a:SUBLANES * a + tm + SUBLANES, lanes]
                part = term if part is None else part + term
            out = out + part[r:r + tm, :]
        return out

    n_chunks = CONV_W // LANES
    gate_cols = wg_ref.shape[1] // n_chunks
    gate_parts, conv_parts = [], []
    for c in range(n_chunks):
        z = jnp.dot(h, wg_ref[:, c * gate_cols:(c + 1) * gate_cols], preferred_element_type=F32)
        gate_parts.append(_sigmoid(z))
        conv_parts.append(conv_lanes(slice(c * LANES, (c + 1) * LANES)))
    gates = jnp.concatenate(gate_parts, axis=1)
    conv = jnp.concatenate(conv_parts, axis=1)

    mu = jnp.mean(conv, axis=-1, keepdims=True)
    cen = conv - mu
    var = jnp.mean(cen * cen, axis=-1, keepdims=True)
    y = (cen * lax.rsqrt(var + 1e-5)) * lg_ref[...] + lb_ref[...]
    act = (y * _sigmoid(y)).astype(BF16)

    y_conv = jnp.dot(act, wco_ref[...], preferred_element_type=F32)
    y_attn = jnp.dot(o_ref[0], wao_ref[...], preferred_element_type=F32)
    mix = (gates[:, 0:d] * y_attn + gates[:, d:2 * d] * y_conv).astype(BF16)
    mixed = jnp.dot(mix, wout_ref[...], preferred_element_type=F32)
    x1_ref[0] = x + (1.0 + mod_ref[0, 2:3, :]) * mixed


def _mixer_call(x, mod, h, wg, o, u, cw, cb, lg, lb, wao, wco, wout):
    b, s, d = x.shape
    tm = min(ROW_TILE, s)
    halo_blocks = tm // CONV_HALO
    row = lambda bi, i: (bi, i, 0)
    return pl.pallas_call(
        functools.partial(_mixer_kernel, tm=tm),
        grid=(b, s // tm),
        in_specs=[
            pl.BlockSpec((1, tm, d), row),
            pl.BlockSpec((1, N_MOD, d), lambda bi, i: (bi, 0, 0)),
            pl.BlockSpec((1, tm, d), row),
            _resident(wg.shape),
            pl.BlockSpec((1, tm, ATTN_W), row),
            pl.BlockSpec((1, tm, CONV_W), row),
            pl.BlockSpec((1, CONV_HALO, CONV_W), lambda bi, i: (bi, jnp.maximum(i * halo_blocks - 1, 0), 0)),
            _resident(cw.shape), _resident(cb.shape), _resident(lg.shape), _resident(lb.shape),
            _resident(wao.shape), _resident(wco.shape), _resident(wout.shape),
        ],
        out_specs=pl.BlockSpec((1, tm, d), row),
        out_shape=jax.ShapeDtypeStruct((b, s, d), F32),
        scratch_shapes=[pltpu.VMEM((CONV_HALO + tm + SUBLANES, CONV_W), F32)],
        compiler_params=pltpu.CompilerParams(
            dimension_semantics=("arbitrary", "arbitrary"),
            vmem_limit_bytes=V7X_VMEM_LIMIT_BYTES),
        name="mixer",
    )(x, mod, h, wg, o, u, u, cw, cb, lg, lb, wao, wco, wout)


def _ffn_kernel(x_ref, mod_ref, g_ref, wi_ref, wo_ref, fg_ref, out_ref, *, d_ff, final):
    x = x_ref[0]
    h = _modulated_rms_norm(x, g_ref[...], mod_ref[0, 4:5, :], mod_ref[0, 3:4, :]).astype(BF16)
    acc = jnp.zeros(x.shape, F32)
    for c0 in range(0, d_ff, FFN_CHUNK):
        cw = min(FFN_CHUNK, d_ff - c0)
        f_gate = jnp.dot(h, wi_ref[:, c0:c0 + cw], preferred_element_type=F32)
        f_up = jnp.dot(h, wi_ref[:, d_ff + c0:d_ff + c0 + cw], preferred_element_type=F32)
        a = ((f_gate * _sigmoid(f_gate)) * f_up).astype(BF16)
        acc = acc + jnp.dot(a, wo_ref[c0:c0 + cw, :], preferred_element_type=F32)
    x2 = x + (1.0 + mod_ref[0, 5:6, :]) * acc
    if final:
        x2 = (x2 * lax.rsqrt(jnp.mean(x2 * x2, axis=-1, keepdims=True) + 1e-6)) * fg_ref[...]
    out_ref[0] = x2


def _ffn_call(x, mod, g, wi, wo, final_g, final):
    b, s, d = x.shape
    d_ff = wo.shape[0]
    tm = min(FFN_ROW_TILE, s)
    row = lambda bi, i: (bi, i, 0)
    return pl.pallas_call(
        functools.partial(_ffn_kernel, d_ff=d_ff, final=final),
        grid=(b, s // tm),
        in_specs=[
            pl.BlockSpec((1, tm, d), row),
            pl.BlockSpec((1, N_MOD, d), lambda bi, i: (bi, 0, 0)),
            _resident((1, d)),
            _resident(wi.shape),
            _resident(wo.shape),
            _resident((1, d)),
        ],
        out_specs=pl.BlockSpec((1, tm, d), row),
        out_shape=jax.ShapeDtypeStruct((b, s, d), F32),
        compiler_params=pltpu.CompilerParams(
            dimension_semantics=("arbitrary", "arbitrary"),
            vmem_limit_bytes=V7X_VMEM_LIMIT_BYTES),
        name="ffn_final" if final else "ffn",
    )(x, mod, g, wi, wo, final_g)


def _rope_tables(positions):
    inv_freq = 1.0 / (ROPE_THETA ** (jnp.arange(0, ROT_DIM, 2, dtype=F32) / ROT_DIM))
    ang = positions.astype(F32)[..., None] * inv_freq
    return jnp.swapaxes(jnp.cos(ang), 1, 2), jnp.swapaxes(jnp.sin(ang), 1, 2)


def kernel(x, c, positions, ada_w, ada_b, norm1_g, w_in, lambda_q1, lambda_k1, lambda_q2, lambda_k2,
           subln_g, w_attn_o, dw_conv_w, dw_conv_b, conv_ln_g, conv_ln_b, w_conv_o, w_out, norm2_g,
           w_ffn_in, w_ffn_out, final_g):
    b, s, d = x.shape
    depth = ada_w.shape[0]
    assert all(s % min(t, s) == 0 for t in (ROW_TILE, FFN_ROW_TILE, INPROJ_ROW_TILE, ATTN_TILE))
    assert min(INPROJ_ROW_TILE, s) % min(ATTN_TILE, s) == 0
    assert (2 * min(ATTN_TILE, s)) % ATTN_GROUP == 0
    assert ada_w.shape[2] == N_MOD * d and ada_w.shape[2] % ADA_COLS == 0
    assert w_in.shape[2] == N_QKVU + 2 * d

    rows = -(-b // 8) * 8
    c_pad = jnp.zeros((rows, d), F32).at[:b].set(c)
    mod_all = _ada_call(c_pad, ada_w, ada_b)[:, :b].reshape(depth, b, N_MOD, d)

    cos_t, sin_t = _rope_tables(positions)
    final_row = final_g.reshape(1, d)

    for l in range(depth):
        lambda_init = 0.8 - 0.6 * math.exp(-0.3 * l)
        mod = mod_all[l]
        g1 = norm1_g[l].reshape(1, d)
        w_l = w_in[l]
        qt, k, vt, u, h = _inproj_call(x, mod, g1, w_l[:, :3 * ATTN_W].T.astype(BF16),
                                    w_l[:, 3 * ATTN_W:N_QKVU].astype(BF16), cos_t, sin_t)
        o = _attn_call(qt, k, vt,
                       lambda_q1[l].reshape(1, HEAD_DIM), lambda_k1[l].reshape(1, HEAD_DIM),
                       lambda_q2[l].reshape(1, HEAD_DIM), lambda_k2[l].reshape(1, HEAD_DIM),
                       subln_g[l].reshape(HEAD_W, 1), lambda_init)
        x = _mixer_call(x, mod, h, w_l[:, N_QKVU:].astype(BF16), o, u,
                        dw_conv_w[l].reshape(CONV_K, CONV_W), dw_conv_b[l].reshape(1, CONV_W),
                        conv_ln_g[l].reshape(1, CONV_W), conv_ln_b[l].reshape(1, CONV_W),
                        w_attn_o[l].astype(BF16), w_conv_o[l].astype(BF16), w_out[l].astype(BF16))
        x = _ffn_call(x, mod, norm2_g[l].reshape(1, d), w_ffn_in[l].astype(BF16),
                      w_ffn_out[l].astype(BF16), final_row, final=(l == depth - 1))
    return x
```
